```python
import math
import jax, jax.numpy as jnp
from jax import lax
import numpy as np

D_MODEL = 1024
BATCH = 4
SEQ = 8192
DEPTH = 4

GRID_W = 64
CTX_LEN = 256
N_MOD = 9
D_FF = 2816
MACARON = 0.5
A_HEADS = 8
QK_NOPE = 64
QK_ROPE = 32
V_DIM = 64
Q_RANK = 384
KV_RANK = 256
A_WIDTH = A_HEADS * V_DIM
ATTN_SCALE = 1.0 / math.sqrt(QK_NOPE + QK_ROPE)
ROPE_BASE = 10000.0
Q_BLOCK = 128
CHUNK = 128
B_GROUPS = 4
B_GROUP_CH = 128
B_WIDTH = B_GROUPS * B_GROUP_CH
OFF_KV = Q_RANK
OFF_KR = Q_RANK + KV_RANK
OFF_B = OFF_KR + QK_ROPE
IN_E = OFF_B + 2 * B_WIDTH
CONV_W = 31
LN_EPS = 1e-5
RMS_EPS = 1e-6
ALPHA = (2 * DEPTH) ** 0.25
BETA = (8 * DEPTH) ** -0.25
N_EVEN = (DEPTH + 1) // 2
N_ODD = DEPTH // 2

kernel_name = "hybrid_mla_gmlp_conformer_dit"


def layer_norm(x, g, b):
    xf = x.astype(jnp.float32)
    mu = jnp.mean(xf, axis=-1, keepdims=True)
    var = jnp.mean(jnp.square(xf - mu), axis=-1, keepdims=True)
    return ((xf - mu) * lax.rsqrt(var + LN_EPS)).astype(x.dtype) * g + b


def rms_norm(x, g):
    xf = x.astype(jnp.float32)
    return (xf * lax.rsqrt(jnp.mean(xf * xf, axis=-1, keepdims=True) + RMS_EPS)).astype(x.dtype) * g


def adaln(cond, w_mod, b_mod):
    m = jax.nn.silu(cond) @ w_mod + b_mod
    return m.reshape(cond.shape[:-1] + (N_MOD, D_MODEL))


def modulate(h, mod, idx):
    return h * (1 + mod[:, 3 * idx + 1][:, None]) + mod[:, 3 * idx][:, None]


def residual(h, y, mod, idx, g, b, weight):
    return layer_norm(ALPHA * h + weight * mod[:, 3 * idx + 2][:, None] * y, g, b)


def swiglu(h, w13, w2):
    gt, up = jnp.split(h @ w13, 2, axis=-1)
    return (jax.nn.silu(gt) * up) @ w2


def ffn_sublayer(h, mod, idx, w13, w2, g, b):
    return residual(h, swiglu(modulate(h, mod, idx), w13, w2), mod, idx, g, b, MACARON)


def axial_rope(rows, cols):
    half = QK_ROPE // 2
    inv = 1.0 / (ROPE_BASE ** (jnp.arange(0, half, 2, dtype=jnp.float32) / half))
    ang = jnp.concatenate([rows[:, None] * inv, cols[:, None] * inv], axis=-1)
    return jnp.cos(ang), jnp.sin(ang)


def apply_rope(x, cos, sin):
    xp = x.reshape(x.shape[:-1] + (QK_ROPE // 2, 2))
    x0, x1 = xp[..., 0], xp[..., 1]
    extra = (1,) * (x.ndim - 3)
    c = cos.reshape(cos.shape[:1] + extra + cos.shape[1:]).astype(x.dtype)
    s = sin.reshape(sin.shape[:1] + extra + sin.shape[1:]).astype(x.dtype)
    return jnp.stack([x0 * c - x1 * s, x0 * s + x1 * c], axis=-1).reshape(x.shape)


def mla_queries(cq, q_norm_g, w_q_up):
    q = (rms_norm(cq, q_norm_g) @ w_q_up).reshape(cq.shape[:2] + (A_HEADS, QK_NOPE + QK_ROPE))
    return q[..., :QK_NOPE], q[..., QK_NOPE:]


def mla_keys_values(ckv, kv_norm_g, w_kv_up):
    kv = (rms_norm(ckv, kv_norm_g) @ w_kv_up).reshape(ckv.shape[:2] + (A_HEADS, QK_NOPE + V_DIM))
    return kv[..., :QK_NOPE], kv[..., QK_NOPE:]


def mla_attend(q_nope, q_rope, k_nope, k_rope, v):
    s = (jnp.einsum('bqhd,bkhd->bhqk', q_nope, k_nope)
         + jnp.einsum('bqhr,bkr->bhqk', q_rope, k_rope)) * ATTN_SCALE
    p = jax.nn.softmax(s.astype(jnp.float32), axis=-1).astype(v.dtype)
    return jnp.einsum('bhqk,bkhd->bqhd', p, v)


def mla_latent_attention(q_nope, q_rope, k_nope, k_rope, v):
    bn, length = q_nope.shape[:2]
    nb = length // Q_BLOCK

    def to_blocks(t):
        return jnp.moveaxis(t.reshape((bn, nb, Q_BLOCK) + t.shape[2:]), 1, 0)

    out = lax.map(lambda qb: mla_attend(qb[0], qb[1], k_nope, k_rope, v),
                  (to_blocks(q_nope), to_blocks(q_rope)))
    return jnp.moveaxis(out, 0, 1).reshape(bn, length, A_WIDTH)


def spatial_gating(zb, sgu_g, sgu_b, w_s, b_s):
    u, v = jnp.split(jax.nn.gelu(zb, approximate=False), 2, axis=-1)
    v = layer_norm(v, sgu_g, sgu_b)
    bn, length = v.shape[:2]
    vc = v.reshape(bn, length // CHUNK, CHUNK, B_GROUPS, B_GROUP_CH)
    mixed = jnp.einsum('gij,bnjgc->bnigc', w_s, vc) + b_s.T[:, :, None]
    return u * mixed.reshape(bn, length, B_WIDTH)


def even_mixer(h_lat, h_ctx, ctx_out, cos, sin, w_in, q_norm_g, kv_norm_g, w_q_up, w_kv_up,
               sgu_g, sgu_b, w_s, b_s, w_out):
    cq, ckv, kr, zb = jnp.split(h_lat @ w_in, [OFF_KV, OFF_KR, OFF_B], axis=-1)
    if ctx_out:
        cq_c, ckv_c, kr_c, zb_c = jnp.split(h_ctx @ w_in, [OFF_KV, OFF_KR, OFF_B], axis=-1)
    else:
        ckv_c, kr_c = jnp.split(h_ctx @ w_in[:, OFF_KV:OFF_B], [KV_RANK], axis=-1)
    kn_c, v_c = mla_keys_values(ckv_c, kv_norm_g, w_kv_up)
    kn_l, v_l = mla_keys_values(ckv, kv_norm_g, w_kv_up)
    kr_l = apply_rope(kr, cos, sin)
    qn_l, qr_l = mla_queries(cq, q_norm_g, w_q_up)
    qr_l = apply_rope(qr_l, cos, sin)
    k_nope = jnp.concatenate([kn_c, kn_l], axis=1)
    k_rope = jnp.concatenate([kr_c, kr_l], axis=1)
    v_all = jnp.concatenate([v_c, v_l], axis=1)
    a_lat = mla_latent_attention(qn_l, qr_l, k_nope, k_rope, v_all)
    b_lat = spatial_gating(zb, sgu_g, sgu_b, w_s, b_s)
    y_lat = jnp.concatenate([a_lat, b_lat], axis=-1) @ w_out
    if not ctx_out:
        return y_lat, None
    qn_c, qr_c = mla_queries(cq_c, q_norm_g, w_q_up)
    a_ctx = mla_attend(qn_c, qr_c, kn_c, kr_c, v_c).reshape(h_ctx.shape[0], h_ctx.shape[1], A_WIDTH)
    b_ctx = spatial_gating(zb_c, sgu_g, sgu_b, w_s, b_s)
    y_ctx = jnp.concatenate([a_ctx, b_ctx], axis=-1) @ w_out
    return y_lat, y_ctx


def conv_mixer(h, w_pw1, b_pw1, w_dw, b_dw, cg, cb, w_out, b_out):
    a, gt = jnp.split(h @ w_pw1 + b_pw1, 2, axis=-1)
    y = a * jax.nn.sigmoid(gt)
    y = lax.conv_general_dilated(y, w_dw[:, None, :], window_strides=(1,), padding='SAME',
                                 dimension_numbers=('NWC', 'WIO', 'NWC'),
                                 feature_group_count=D_MODEL) + b_dw
    y = jax.nn.silu(layer_norm(y, cg, cb))
    return y @ w_out + b_out


def _ctx_needed(l):
    return any(j % 2 == 0 for j in range(l, DEPTH))


def setup_inputs(seed: int = 0) -> dict:
    key = jax.random.key(seed)
    ks = iter(jax.random.split(key, 40))

    def nrm(shape, s):
        return jax.random.normal(next(ks), shape, jnp.float32) * s

    D = D_MODEL
    kv_scale = jnp.concatenate([jnp.ones((QK_NOPE,), jnp.float32), jnp.full((V_DIM,), BETA, jnp.float32)])
    w_kv_up = (nrm((N_EVEN, KV_RANK, A_HEADS, QK_NOPE + V_DIM), KV_RANK ** -0.5) * kv_scale
               ).reshape(N_EVEN, KV_RANK, A_HEADS * (QK_NOPE + V_DIM))
    return {
        "x": nrm((BATCH, SEQ, D), 1.0),
        "c": nrm((BATCH, D), 1.0),
        "ctx": nrm((BATCH, CTX_LEN, D), 1.0),
        "c_ctx": nrm((D,), 1.0),
        "w_mod": nrm((DEPTH, D, N_MOD * D), 0.5 * D ** -0.5),
        "b_mod": nrm((DEPTH, N_MOD * D), 0.02),
        "ln_g": 1.0 + nrm((DEPTH, 3, D), 0.02),
        "ln_b": nrm((DEPTH, 3, D), 0.02),
        "ffn_w13": nrm((DEPTH, 2, D, 2 * D_FF), D ** -0.5),
        "ffn_w2": nrm((DEPTH, 2, D_FF, D), BETA * D_FF ** -0.5),
        "e_w_in": nrm((N_EVEN, D, IN_E), D ** -0.5),
        "e_q_norm": 1.0 + nrm((N_EVEN, Q_RANK), 0.02),
        "e_kv_norm": 1.0 + nrm((N_EVEN, KV_RANK), 0.02),
        "e_w_q_up": nrm((N_EVEN, Q_RANK, A_HEADS * (QK_NOPE + QK_ROPE)), Q_RANK ** -0.5),
        "e_w_kv_up": w_kv_up,
        "e_sgu_g": 1.0 + nrm((N_EVEN, B_WIDTH), 0.02),
        "e_sgu_b": nrm((N_EVEN, B_WIDTH), 0.02),
        "e_w_s": nrm((N_EVEN, B_GROUPS, CHUNK, CHUNK), CHUNK ** -0.5),
        "e_b_s": 1.0 + nrm((N_EVEN, B_GROUPS, CHUNK), 0.02),
        "e_w_out": nrm((N_EVEN, D, D), BETA * D ** -0.5),
        "o_w_pw1": nrm((N_ODD, D, 2 * D), D ** -0.5),
        "o_b_pw1": nrm((N_ODD, 2 * D), 0.02),
        "o_w_dw": nrm((N_ODD, CONV_W, D), CONV_W ** -0.5),
        "o_b_dw": nrm((N_ODD, D), 0.02),
        "o_ln_g": 1.0 + nrm((N_ODD, D), 0.02),
        "o_ln_b": nrm((N_ODD, D), 0.02),
        "o_w_out": nrm((N_ODD, D, D), BETA * D ** -0.5),
        "o_b_out": nrm((N_ODD, D), 0.02),
    }


def reference(x, c, ctx, c_ctx, w_mod, b_mod, ln_g, ln_b, ffn_w13, ffn_w2,
              e_w_in, e_q_norm, e_kv_norm, e_w_q_up, e_w_kv_up, e_sgu_g, e_sgu_b, e_w_s, e_b_s, e_w_out,
              o_w_pw1, o_b_pw1, o_w_dw, o_b_dw, o_ln_g, o_ln_b, o_w_out, o_b_out):
    length = x.shape[1]
    ROWS = length // GRID_W
    rows = jnp.repeat(jnp.arange(ROWS, dtype=jnp.float32), GRID_W)
    cols = jnp.tile(jnp.arange(GRID_W, dtype=jnp.float32), ROWS)
    cos, sin = axial_rope(rows, cols)

    lat, cx = x, ctx
    for l in range(DEPTH):
        ctx_in = _ctx_needed(l)
        ctx_out = _ctx_needed(l + 1)
        m_lat = adaln(c, w_mod[l], b_mod[l])
        lat = ffn_sublayer(lat, m_lat, 0, ffn_w13[l, 0], ffn_w2[l, 0], ln_g[l, 0], ln_b[l, 0])
        h_lat = modulate(lat, m_lat, 1)
        h_ctx = None
        if ctx_in:
            m_ctx = adaln(c_ctx[None], w_mod[l], b_mod[l])
            cx = ffn_sublayer(cx, m_ctx, 0, ffn_w13[l, 0], ffn_w2[l, 0], ln_g[l, 0], ln_b[l, 0])
            h_ctx = modulate(cx, m_ctx, 1)
        if l % 2 == 0:
            e = l // 2
            y_lat, y_ctx = even_mixer(h_lat, h_ctx, ctx_out, cos, sin, e_w_in[e], e_q_norm[e], e_kv_norm[e],
                                      e_w_q_up[e], e_w_kv_up[e], e_sgu_g[e], e_sgu_b[e], e_w_s[e], e_b_s[e],
                                      e_w_out[e])
        else:
            o = l // 2
            conv_args = (o_w_pw1[o], o_b_pw1[o], o_w_dw[o], o_b_dw[o], o_ln_g[o], o_ln_b[o], o_w_out[o], o_b_out[o])
            y_lat = conv_mixer(h_lat, *conv_args)
            y_ctx = conv_mixer(h_ctx, *conv_args) if ctx_out else None
        lat = residual(lat, y_lat, m_lat, 1, ln_g[l, 1], ln_b[l, 1], 1.0)
        lat = ffn_sublayer(lat, m_lat, 2, ffn_w13[l, 1], ffn_w2[l, 1], ln_g[l, 2], ln_b[l, 2])
        if ctx_out:
            cx = residual(cx, y_ctx, m_ctx, 1, ln_g[l, 1], ln_b[l, 1], 1.0)
            cx = ffn_sublayer(cx, m_ctx, 2, ffn_w13[l, 1], ffn_w2[l, 1], ln_g[l, 2], ln_b[l, 2])
    return lat
```

```python
import functools
import math

import jax
import jax.numpy as jnp
from jax import lax
from jax.experimental import pallas as pl
from jax.experimental.pallas import tpu as pltpu

N_MOD = 9
D_FF = 2816
MACARON = 0.5
A_HEADS = 8
QK_NOPE = 64
QK_ROPE = 32
V_DIM = 64
Q_RANK = 384
KV_RANK = 256
ATTN_SCALE = 1.0 / math.sqrt(QK_NOPE + QK_ROPE)
ROPE_BASE = 10000.0
GRID_W = 64
CHUNK = 128
B_GROUPS = 4
B_WIDTH = 512
CONV_W = 31
LN_EPS = 1e-5
RMS_EPS = 1e-6

LANES = 128
SUBLANES = 8
HEAD_PAD = 128
V7X_VMEM_LIMIT = 56 * 1024 * 1024

CONV_HALO = 16
CONV_ROWS = 32

BF16 = jnp.bfloat16
F32 = jnp.float32


def _dot(a, b):
    return jnp.dot(a, b, preferred_element_type=F32)


def _dot_nt(a, b):
    return lax.dot_general(a, b, (((1,), (1,)), ((), ())), preferred_element_type=F32)


def _layer_norm(x, g, b):
    mu = jnp.mean(x, axis=-1, keepdims=True)
    xc = x - mu
    var = jnp.mean(xc * xc, axis=-1, keepdims=True)
    return xc * lax.rsqrt(var + LN_EPS) * g + b


def _rms_norm(x, g):
    return x * lax.rsqrt(jnp.mean(x * x, axis=-1, keepdims=True) + RMS_EPS) * g


def _modulate(x, m, idx):
    return x * (1.0 + m[3 * idx + 1:3 * idx + 2]) + m[3 * idx:3 * idx + 1]


def _params(n_grid):
    return pltpu.CompilerParams(dimension_semantics=("arbitrary",) * n_grid,
                                vmem_limit_bytes=V7X_VMEM_LIMIT)


def _const_spec(shape):
    return pl.BlockSpec(shape, lambda *_: (0,) * len(shape), pipeline_mode=pl.Buffered(1))


def _row_tile(length, want):
    return min(want, length)


def _adaln_kernel(c_ref, w_ref, b_ref, o_ref):
    a = jax.nn.silu(c_ref[...]).astype(BF16)
    o_ref[0] = _dot(a, w_ref[0].astype(BF16)) + b_ref[0]


def _adaln(conds, w_mod, b_mod, tn=1152):
    depth, d, n = w_mod.shape
    rows = conds.shape[0]
    return pl.pallas_call(
        _adaln_kernel,
        grid=(depth, n // tn),
        in_specs=[pl.BlockSpec((rows, d), lambda l, j: (0, 0)),
                  pl.BlockSpec((1, d, tn), lambda l, j: (l, 0, j)),
                  pl.BlockSpec((1, 1, tn), lambda l, j: (l, 0, j))],
        out_specs=pl.BlockSpec((1, rows, tn), lambda l, j: (l, 0, j)),
        out_shape=jax.ShapeDtypeStruct((depth, rows, n), F32),
        compiler_params=_params(2),
        name="adaln",
    )(conds, w_mod, b_mod.reshape(depth, 1, n))


def _ffn_kernel(x_ref, mod_ref, w13_ref, w2_ref, g_ref, b_ref, o_ref, *, idx, alpha, chunk):
    x = x_ref[0]
    m = mod_ref[0]
    xm = _modulate(x, m, idx).astype(BF16)
    acc = jnp.zeros(x.shape, F32)
    for c in range(D_FF // chunk):
        gt = _dot(xm, w13_ref[:, c * chunk:(c + 1) * chunk])
        up = _dot(xm, w13_ref[:, D_FF + c * chunk:D_FF + (c + 1) * chunk])
        h = (gt * jax.nn.sigmoid(gt) * up).astype(BF16)
        acc = acc + _dot(h, w2_ref[c * chunk:(c + 1) * chunk, :])
    r = alpha * x + (MACARON * m[3 * idx + 2:3 * idx + 3]) * acc
    o_ref[0] = _layer_norm(r, g_ref[...], b_ref[...])


def _ffn(x, mods, mod_row, idx, w13, w2, g, b, alpha, tm=512, chunk=256):
    bsz, length, d = x.shape
    tm = _row_tile(length, tm)
    kernel = functools.partial(_ffn_kernel, idx=idx, alpha=alpha, chunk=chunk)
    return pl.pallas_call(
        kernel,
        grid=(bsz, length // tm),
        in_specs=[pl.BlockSpec((1, tm, d), lambda bi, i: (bi, i, 0)),
                  pl.BlockSpec((1, N_MOD, d), lambda bi, i: (mod_row(bi), 0, 0)),
                  _const_spec(w13.shape), _const_spec(w2.shape),
                  _const_spec((1, d)), _const_spec((1, d))],
        out_specs=pl.BlockSpec((1, tm, d), lambda bi, i: (bi, i, 0)),
        out_shape=jax.ShapeDtypeStruct(x.shape, F32),
        compiler_params=_params(2),
        name="ffn",
    )(x, mods, w13, w2, g.reshape(1, d), b.reshape(1, d))


def _proj_kernel(x_ref, mod_ref, win_ref, qg_ref, kvg_ref, wq_ref, wk_ref, wvt_ref, c2_ref, s2_ref,
                 sg_ref, sb_ref, ws_ref, bs_ref, q_ref, k_ref, vt_ref, bg_ref):
    x = x_ref[0]
    tm = x.shape[0]
    h = _modulate(x, mod_ref[0], 1).astype(BF16)
    z = _dot(h, win_ref[...])
    o_kv = Q_RANK
    o_kr = o_kv + KV_RANK
    o_sw = o_kr + HEAD_PAD
    o_b = o_sw + HEAD_PAD
    cq, ckv = z[:, :o_kv], z[:, o_kv:o_kr]
    zkr, zsw, zb = z[:, o_kr:o_sw], z[:, o_sw:o_b], z[:, o_b:]

    c2 = c2_ref[...]
    s2 = s2_ref[...]
    width = A_HEADS * HEAD_PAD
    cqn = _rms_norm(cq, qg_ref[...]).astype(BF16)
    qab = _dot(cqn, wq_ref[...])
    q = qab[:, :width] * jnp.tile(c2, (1, A_HEADS)) + qab[:, width:] * jnp.tile(s2, (1, A_HEADS))
    q_ref[0] = (q * ATTN_SCALE).astype(BF16)

    ckvn = _rms_norm(ckv, kvg_ref[...]).astype(BF16)
    k_rope = zkr * c2 + zsw * s2
    k = _dot(ckvn, wk_ref[...]) + jnp.tile(k_rope, (1, A_HEADS))
    k_ref[0] = k.astype(BF16)
    vt_ref[0] = _dot_nt(wvt_ref[...], ckvn).astype(BF16)

    gz = 0.5 * zb * (1.0 + lax.erf(zb * math.sqrt(0.5)))
    u, v = gz[:, :B_WIDTH], gz[:, B_WIDTH:]
    vn = _layer_norm(v, sg_ref[...], sb_ref[...]).astype(BF16)
    gch = B_WIDTH // B_GROUPS
    for n in range(tm // CHUNK):
        rows = slice(n * CHUNK, (n + 1) * CHUNK)
        for g in range(B_GROUPS):
            cols = slice(g * gch, (g + 1) * gch)
            mixed = _dot(ws_ref[g], vn[rows, cols]) + bs_ref[g]
            bg_ref[0, rows, cols] = (u[rows, cols] * mixed).astype(BF16)


def _proj(x, mods, mod_row, ew, c2, s2, tm=256):
    bsz, length, d = x.shape
    tm = _row_tile(length, tm)
    width = A_HEADS * HEAD_PAD
    row = lambda bi, i: (bi, i, 0)
    return pl.pallas_call(
        _proj_kernel,
        grid=(bsz, length // tm),
        in_specs=[pl.BlockSpec((1, tm, d), row),
                  pl.BlockSpec((1, N_MOD, d), lambda bi, i: (mod_row(bi), 0, 0)),
                  _const_spec(ew["w_in"].shape), _const_spec((1, Q_RANK)), _const_spec((1, KV_RANK)),
                  _const_spec(ew["wq"].shape), _const_spec(ew["wk"].shape), _const_spec(ew["wvt"].shape),
                  pl.BlockSpec((tm, HEAD_PAD), lambda bi, i: (i, 0)),
                  pl.BlockSpec((tm, HEAD_PAD), lambda bi, i: (i, 0)),
                  _const_spec((1, B_WIDTH)), _const_spec((1, B_WIDTH)),
                  _const_spec(ew["w_s"].shape), _const_spec(ew["b_s"].shape)],
        out_specs=[pl.BlockSpec((1, tm, width), row),
                   pl.BlockSpec((1, tm, width), row),
                   pl.BlockSpec((1, A_HEADS * V_DIM, tm), lambda bi, i: (bi, 0, i)),
                   pl.BlockSpec((1, tm, B_WIDTH), row)],
        out_shape=[jax.ShapeDtypeStruct((bsz, length, width), BF16),
                   jax.ShapeDtypeStruct((bsz, length, width), BF16),
                   jax.ShapeDtypeStruct((bsz, A_HEADS * V_DIM, length), BF16),
                   jax.ShapeDtypeStruct((bsz, length, B_WIDTH), BF16)],
        compiler_params=_params(2),
        name="proj_even",
    )(x, mods, ew["w_in"], ew["q_norm"], ew["kv_norm"], ew["wq"], ew["wk"], ew["wvt"], c2, s2,
      ew["sgu_g"], ew["sgu_b"], ew["w_s"], ew["b_s"])


def _attn_kernel(q_ref, k_ref, vt_ref, o_ref):
    s = _dot_nt(k_ref[0], q_ref[0])
    m = jnp.max(s, axis=0, keepdims=True)
    p = jnp.exp(s - m)
    l = jnp.sum(p, axis=0, keepdims=True)
    o = _dot(vt_ref[0], p.astype(BF16))
    o_ref[0] = o / l


def _attention(q, k, vt, tq=256):
    bsz, lq, _ = q.shape
    lk = k.shape[1]
    tq = _row_tile(lq, tq)
    return pl.pallas_call(
        _attn_kernel,
        grid=(bsz, A_HEADS, lq // tq),
        in_specs=[pl.BlockSpec((1, tq, HEAD_PAD), lambda bi, h, i: (bi, i, h)),
                  pl.BlockSpec((1, lk, HEAD_PAD), lambda bi, h, i: (bi, 0, h)),
                  pl.BlockSpec((1, V_DIM, lk), lambda bi, h, i: (bi, h, 0))],
        out_specs=pl.BlockSpec((1, V_DIM, tq), lambda bi, h, i: (bi, h, i)),
        out_shape=jax.ShapeDtypeStruct((bsz, A_HEADS * V_DIM, lq), F32),
        compiler_params=_params(3),
        name="attention",
    )(q, k, vt)


def _outres_kernel(x_ref, mod_ref, at_ref, bg_ref, woa_ref, wob_ref, g_ref, b_ref, o_ref, *, alpha):
    a = at_ref[0].T.astype(BF16)
    y = _dot(a, woa_ref[...]) + _dot(bg_ref[0], wob_ref[...])
    r = alpha * x_ref[0] + mod_ref[0][5:6] * y
    o_ref[0] = _layer_norm(r, g_ref[...], b_ref[...])


def _outres(x, mods, mod_row, at, bg, woa, wob, g, b, alpha, tm=512):
    bsz, length, d = x.shape
    tm = _row_tile(length, tm)
    row = lambda bi, i: (bi, i, 0)
    return pl.pallas_call(
        functools.partial(_outres_kernel, alpha=alpha),
        grid=(bsz, length // tm),
        in_specs=[pl.BlockSpec((1, tm, d), row),
                  pl.BlockSpec((1, N_MOD, d), lambda bi, i: (mod_row(bi), 0, 0)),
                  pl.BlockSpec((1, A_HEADS * V_DIM, tm), lambda bi, i: (bi, 0, i)),
                  pl.BlockSpec((1, tm, B_WIDTH), row),
                  _const_spec(woa.shape), _const_spec(wob.shape),
                  _const_spec((1, d)), _const_spec((1, d))],
        out_specs=pl.BlockSpec((1, tm, d), row),
        out_shape=jax.ShapeDtypeStruct(x.shape, F32),
        compiler_params=_params(2),
        name="outres_even",
    )(x, mods, at, bg, woa, wob, g.reshape(1, d), b.reshape(1, d))


def _conv_kernel(x_ref, xp_ref, xn_ref, mod_ref, wpw_ref, bpw_ref, wdw_ref, bdw_ref, cg_ref, cb_ref,
                 wo_ref, bo_ref, g_ref, b_ref, o_ref, y_ref, c_ref, *, alpha):
    i = pl.program_id(1)
    last = pl.num_programs(1) - 1
    x = x_ref[0]
    tm, d = x.shape
    m = mod_ref[0]
    xcat = jnp.concatenate([xp_ref[0], x, xn_ref[0]], axis=0)
    h = _modulate(xcat, m, 1).astype(BF16)
    z = _dot(h, wpw_ref[...]) + bpw_ref[...]
    y = z[:, :d] * jax.nn.sigmoid(z[:, d:])
    r = lax.broadcasted_iota(jnp.int32, (tm + 2 * CONV_HALO, 1), 0)
    pad = ((r < CONV_HALO) & (i == 0)) | ((r >= tm + CONV_HALO) & (i == last))
    y_ref[...] = jnp.where(pad, 0.0, y)

    off = CONV_HALO - CONV_W // 2
    for rb in range(tm // CONV_ROWS):
        r0 = rb * CONV_ROWS
        acc = jnp.zeros((CONV_ROWS, d), F32)
        for t in range(CONV_W):
            acc = acc + wdw_ref[t:t + 1, :] * y_ref[r0 + off + t:r0 + off + t + CONV_ROWS, :]
        c_ref[r0:r0 + CONV_ROWS, :] = acc + bdw_ref[...]

    t2 = jax.nn.silu(_layer_norm(c_ref[...], cg_ref[...], cb_ref[...])).astype(BF16)
    y2 = _dot(t2, wo_ref[...]) + bo_ref[...]
    res = alpha * x + m[5:6] * y2
    o_ref[0] = _layer_norm(res, g_ref[...], b_ref[...])


def _convmix(x, mods, mod_row, ow, g, b, alpha, tm=512):
    bsz, length, d = x.shape
    tm = _row_tile(length, tm)
    per = tm // CONV_HALO
    n_halo = length // CONV_HALO
    row = lambda bi, i: (bi, i, 0)
    return pl.pallas_call(
        functools.partial(_conv_kernel, alpha=alpha),
        grid=(bsz, length // tm),
        in_specs=[pl.BlockSpec((1, tm, d), row),
                  pl.BlockSpec((1, CONV_HALO, d), lambda bi, i: (bi, jnp.maximum(i * per - 1, 0), 0)),
                  pl.BlockSpec((1, CONV_HALO, d),
                               lambda bi, i: (bi, jnp.minimum((i + 1) * per, n_halo - 1), 0)),
                  pl.BlockSpec((1, N_MOD, d), lambda bi, i: (mod_row(bi), 0, 0)),
                  _const_spec(ow["w_pw1"].shape), _const_spec((1, 2 * d)),
                  _const_spec((CONV_W, d)), _const_spec((1, d)), _const_spec((1, d)), _const_spec((1, d)),
                  _const_spec(ow["w_out"].shape), _const_spec((1, d)),
                  _const_spec((1, d)), _const_spec((1, d))],
        out_specs=pl.BlockSpec((1, tm, d), row),
        out_shape=jax.ShapeDtypeStruct(x.shape, F32),
        scratch_shapes=[pltpu.VMEM((tm + 2 * CONV_HALO, d), F32), pltpu.VMEM((tm, d), F32)],
        compiler_params=_params(2),
        name="conv_odd",
    )(x, x, x, mods, ow["w_pw1"], ow["b_pw1"], ow["w_dw"], ow["b_dw"], ow["ln_g"], ow["ln_b"],
      ow["w_out"], ow["b_out"], g.reshape(1, d), b.reshape(1, d))


def _pad_cols(w, left, total):
    return jnp.pad(w, ((0, 0), (left, total - left - w.shape[1])))


def _even_weights(w_in, q_norm, kv_norm, w_q_up, w_kv_up, sgu_g, sgu_b, w_s, b_s, w_out):
    half = QK_ROPE // 2
    perm = jnp.concatenate([jnp.arange(0, QK_ROPE, 2), jnp.arange(1, QK_ROPE, 2)])
    swap = jnp.concatenate([perm[half:], perm[:half]])
    o_kv = Q_RANK
    o_kr = o_kv + KV_RANK
    o_b = o_kr + QK_ROPE
    kr = w_in[:, o_kr:o_b]
    w_in_ext = jnp.concatenate([w_in[:, :o_kr],
                                _pad_cols(kr[:, perm], QK_NOPE, HEAD_PAD),
                                _pad_cols(kr[:, swap], QK_NOPE, HEAD_PAD),
                                w_in[:, o_b:]], axis=1)
    wq = w_q_up.reshape(Q_RANK, A_HEADS, QK_NOPE + QK_ROPE)
    nope, rope = wq[:, :, :QK_NOPE], wq[:, :, QK_NOPE:]
    zeros_tail = jnp.zeros((Q_RANK, A_HEADS, HEAD_PAD - QK_NOPE - QK_ROPE), w_q_up.dtype)
    wq_a = jnp.concatenate([nope, rope[:, :, perm], zeros_tail], axis=2)
    wq_b = jnp.concatenate([jnp.zeros_like(nope), rope[:, :, swap], zeros_tail], axis=2)
    wq_ab = jnp.concatenate([wq_a.reshape(Q_RANK, -1), wq_b.reshape(Q_RANK, -1)], axis=1)
    wkv = w_kv_up.reshape(KV_RANK, A_HEADS, QK_NOPE + V_DIM)
    wk = jnp.concatenate([wkv[:, :, :QK_NOPE],
                          jnp.zeros((KV_RANK, A_HEADS, HEAD_PAD - QK_NOPE), w_kv_up.dtype)], axis=2)
    wvt = wkv[:, :, QK_NOPE:].reshape(KV_RANK, A_HEADS * V_DIM).T
    a_width = A_HEADS * V_DIM
    return {
        "w_in": w_in_ext.astype(BF16),
        "q_norm": q_norm.reshape(1, Q_RANK), "kv_norm": kv_norm.reshape(1, KV_RANK),
        "wq": wq_ab.astype(BF16), "wk": wk.reshape(KV_RANK, -1).astype(BF16), "wvt": wvt.astype(BF16),
        "sgu_g": sgu_g.reshape(1, B_WIDTH), "sgu_b": sgu_b.reshape(1, B_WIDTH),
        "w_s": w_s.astype(BF16),
        "b_s": jnp.broadcast_to(b_s[:, :, None], (B_GROUPS, CHUNK, B_WIDTH // B_GROUPS)),
        "wo_a": w_out[:a_width].astype(BF16), "wo_b": w_out[a_width:].astype(BF16),
    }


def _rope_tables(length):
    half = QK_ROPE // 2
    rows = jnp.repeat(jnp.arange(length // GRID_W, dtype=F32), GRID_W)
    cols = jnp.tile(jnp.arange(GRID_W, dtype=F32), length // GRID_W)
    inv = 1.0 / (ROPE_BASE ** (jnp.arange(0, half, 2, dtype=F32) / half))
    ang = jnp.concatenate([rows[:, None] * inv, cols[:, None] * inv], axis=-1)
    cos, sin = jnp.cos(ang), jnp.sin(ang)
    tail = jnp.zeros((length, HEAD_PAD - QK_NOPE - QK_ROPE), F32)
    c2 = jnp.concatenate([jnp.ones((length, QK_NOPE), F32), cos, cos, tail], axis=1)
    s2 = jnp.concatenate([jnp.zeros((length, QK_NOPE), F32), -sin, sin, tail], axis=1)
    return c2, s2


def _identity_tables(length):
    c2 = jnp.concatenate([jnp.ones((length, QK_NOPE + QK_ROPE), F32),
                          jnp.zeros((length, HEAD_PAD - QK_NOPE - QK_ROPE), F32)], axis=1)
    return c2, jnp.zeros((length, HEAD_PAD), F32)


def _ctx_needed(l, depth):
    return any(j % 2 == 0 for j in range(l, depth))


def kernel(x, c, ctx, c_ctx, w_mod, b_mod, ln_g, ln_b, ffn_w13, ffn_w2, e_w_in, e_q_norm, e_kv_norm, e_w_q_up, e_w_kv_up, e_sgu_g, e_sgu_b, e_w_s, e_b_s, e_w_out, o_w_pw1, o_b_pw1, o_w_dw, o_b_dw, o_ln_g, o_ln_b, o_w_out, o_b_out):
    bsz, length, d = x.shape
    ctx_len = ctx.shape[1]
    depth = w_mod.shape[0]
    alpha = (2 * depth) ** 0.25

    n_rows = -(-(bsz + 1) // SUBLANES) * SUBLANES
    conds = jnp.zeros((n_rows, d), F32).at[:bsz].set(c).at[bsz].set(c_ctx)
    mods = _adaln(conds, w_mod, b_mod).reshape(depth, n_rows, N_MOD, d)
    lat_row = lambda bi: bi
    ctx_row = lambda bi: bsz

    w13 = ffn_w13.astype(BF16)
    w2 = ffn_w2.astype(BF16)
    c2_lat, s2_lat = _rope_tables(length)
    c2_ctx, s2_ctx = _identity_tables(ctx_len)

    lat, cx = x, ctx
    for l in range(depth):
        ctx_in = _ctx_needed(l, depth)
        ctx_out = _ctx_needed(l + 1, depth)
        ml = mods[l]
        ffn_a = (w13[l, 0], w2[l, 0], ln_g[l, 0], ln_b[l, 0], alpha)
        ffn_b = (w13[l, 1], w2[l, 1], ln_g[l, 2], ln_b[l, 2], alpha)
        lat = _ffn(lat, ml, lat_row, 0, *ffn_a)
        if ctx_in:
            cx = _ffn(cx, ml, ctx_row, 0, *ffn_a)
        if l % 2 == 0:
            e = l // 2
            ew = _even_weights(e_w_in[e], e_q_norm[e], e_kv_norm[e], e_w_q_up[e], e_w_kv_up[e],
                               e_sgu_g[e], e_sgu_b[e], e_w_s[e], e_b_s[e], e_w_out[e])
            q_l, k_l, vt_l, bg_l = _proj(lat, ml, lat_row, ew, c2_lat, s2_lat)
            q_c, k_c, vt_c, bg_c = _proj(cx, ml, ctx_row, ew, c2_ctx, s2_ctx)
            k_all = jnp.concatenate([k_c, k_l], axis=1)
            vt_all = jnp.concatenate([vt_c, vt_l], axis=2)
            at_l = _attention(q_l, k_all, vt_all)
            res = (ew["wo_a"], ew["wo_b"], ln_g[l, 1], ln_b[l, 1], alpha)
            lat = _outres(lat, ml, lat_row, at_l, bg_l, *res)
            if ctx_out:
                at_c = _attention(q_c, k_c, vt_c)
                cx = _outres(cx, ml, ctx_row, at_c, bg_c, *res)
        else:
            o = l // 2
            ow = {"w_pw1": o_w_pw1[o].astype(BF16), "b_pw1": o_b_pw1[o].reshape(1, -1),
                  "w_dw": o_w_dw[o], "b_dw": o_b_dw[o].reshape(1, d),
                  "ln_g": o_ln_g[o].reshape(1, d), "ln_b": o_ln_b[o].reshape(1, d),
                  "w_out": o_w_out[o].astype(BF16), "b_out": o_b_out[o].reshape(1, d)}
            lat = _convmix(lat, ml, lat_row, ow, ln_g[l, 1], ln_b[l, 1], alpha)
            if ctx_out:
                cx = _convmix(cx, ml, ctx_row, ow, ln_g[l, 1], ln_b[l, 1], alpha)
        lat = _ffn(lat, ml, lat_row, 2, *ffn_b)
        if ctx_out:
            cx = _ffn(cx, ml, ctx_row, 2, *ffn_b)
    return lat
```

```python
import functools
import math

import jax
import jax.numpy as jnp
from jax import lax
from jax.experimental import pallas as pl
from jax.experimental.pallas import tpu as pltpu

N_MOD = 9
D_FF = 2816
MACARON = 0.5
A_HEADS = 8
QK_NOPE = 64
QK_ROPE = 32
V_DIM = 64
Q_RANK = 384
KV_RANK = 256
ATTN_SCALE = 1.0 / math.sqrt(QK_NOPE + QK_ROPE)
ROPE_BASE = 10000.0
GRID_W = 64
CHUNK = 128
B_GROUPS = 4
B_WIDTH = 512
CONV_W = 31
LN_EPS = 1e-5
RMS_EPS = 1e-6

LANES = 128
SUBLANES = 8
BF16_ROWS = 16
LOG2_E = math.log2(math.e)
HEAD_PAD = 128
V7X_VMEM_LIMIT = 56 * 1024 * 1024

CONV_HALO = 16
CONV_ROWS = 128

BF16 = jnp.bfloat16
F32 = jnp.float32


def _dot(a, b):
    return jnp.dot(a, b, preferred_element_type=F32)


def _dot_nt(a, b):
    return lax.dot_general(a, b, (((1,), (1,)), ((), ())), preferred_element_type=F32)


def _layer_norm(x, g, b):
    mu = jnp.mean(x, axis=-1, keepdims=True)
    xc = x - mu
    var = jnp.mean(xc * xc, axis=-1, keepdims=True)
    return xc * lax.rsqrt(var + LN_EPS) * g + b


def _rms_norm(x, g):
    return x * lax.rsqrt(jnp.mean(x * x, axis=-1, keepdims=True) + RMS_EPS) * g


def _modulate(x, m, idx):
    return x * (1.0 + m[3 * idx + 1:3 * idx + 2]) + m[3 * idx:3 * idx + 1]


def _params(n_grid):
    return pltpu.CompilerParams(dimension_semantics=("arbitrary",) * n_grid,
                                vmem_limit_bytes=V7X_VMEM_LIMIT)


def _const_spec(shape):
    return pl.BlockSpec(shape, lambda *_: (0,) * len(shape), pipeline_mode=pl.Buffered(1))


def _row_tile(length, want):
    return min(want, length)


def _adaln_kernel(c_ref, w_ref, b_ref, o_ref):
    a = jax.nn.silu(c_ref[...]).astype(BF16)
    o_ref[0] = _dot(a, w_ref[0].astype(BF16)) + b_ref[0]


def _adaln(conds, w_mod, b_mod, tn=1152):
    depth, d, n = w_mod.shape
    rows = conds.shape[0]
    return pl.pallas_call(
        _adaln_kernel,
        grid=(depth, n // tn),
        in_specs=[pl.BlockSpec((rows, d), lambda l, j: (0, 0)),
                  pl.BlockSpec((1, d, tn), lambda l, j: (l, 0, j)),
                  pl.BlockSpec((1, 1, tn), lambda l, j: (l, 0, j))],
        out_specs=pl.BlockSpec((1, rows, tn), lambda l, j: (l, 0, j)),
        out_shape=jax.ShapeDtypeStruct((depth, rows, n), F32),
        compiler_params=_params(2),
        name="adaln",
    )(conds, w_mod, b_mod.reshape(depth, 1, n))


def _ffn_kernel(x_ref, mod_ref, w13_ref, w2_ref, g_ref, b_ref, o_ref, *, idx, alpha, chunk):
    x = x_ref[0]
    m = mod_ref[0]
    xm = _modulate(x, m, idx).astype(BF16)
    acc = jnp.zeros(x.shape, F32)
    for c in range(D_FF // chunk):
        gt = _dot(xm, w13_ref[:, c * chunk:(c + 1) * chunk])
        up = _dot(xm, w13_ref[:, D_FF + c * chunk:D_FF + (c + 1) * chunk])
        h = (gt * jax.nn.sigmoid(gt) * up).astype(BF16)
        acc = acc + _dot(h, w2_ref[c * chunk:(c + 1) * chunk, :])
    r = alpha * x + (MACARON * m[3 * idx + 2:3 * idx + 3]) * acc
    o_ref[0] = _layer_norm(r, g_ref[...], b_ref[...])


def _ffn(x, mods, mod_row, idx, w13, w2, g, b, alpha, tm=512, chunk=256):
    bsz, length, d = x.shape
    tm = _row_tile(length, tm)
    kernel = functools.partial(_ffn_kernel, idx=idx, alpha=alpha, chunk=chunk)
    return pl.pallas_call(
        kernel,
        grid=(bsz, length // tm),
        in_specs=[pl.BlockSpec((1, tm, d), lambda bi, i: (bi, i, 0)),
                  pl.BlockSpec((1, N_MOD, d), lambda bi, i: (mod_row(bi), 0, 0)),
                  _const_spec(w13.shape), _const_spec(w2.shape),
                  _const_spec((1, d)), _const_spec((1, d))],
        out_specs=pl.BlockSpec((1, tm, d), lambda bi, i: (bi, i, 0)),
        out_shape=jax.ShapeDtypeStruct(x.shape, F32),
        compiler_params=_params(2),
        name="ffn",
    )(x, mods, w13, w2, g.reshape(1, d), b.reshape(1, d))


def _proj_kernel(x_ref, mod_ref, win_ref, qg_ref, kvg_ref, wq_ref, wk_ref, wvt_ref, c2_ref, s2_ref,
                 sg_ref, sb_ref, ws_ref, bs_ref, q_ref, k_ref, vt_ref, bg_ref):
    x = x_ref[0]
    tm = x.shape[0]
    h = _modulate(x, mod_ref[0], 1).astype(BF16)
    z = _dot(h, win_ref[...])
    o_kv = Q_RANK
    o_kr = o_kv + KV_RANK
    o_sw = o_kr + HEAD_PAD
    o_b = o_sw + HEAD_PAD
    cq, ckv = z[:, :o_kv], z[:, o_kv:o_kr]
    zkr, zsw, zb = z[:, o_kr:o_sw], z[:, o_sw:o_b], z[:, o_b:]

    c2 = c2_ref[...]
    s2 = s2_ref[...]
    width = A_HEADS * HEAD_PAD
    cqn = _rms_norm(cq, qg_ref[...]).astype(BF16)
    qab = _dot(cqn, wq_ref[...])
    q = qab[:, :width] * jnp.tile(c2, (1, A_HEADS)) + qab[:, width:] * jnp.tile(s2, (1, A_HEADS))
    q_ref[0] = (q * (ATTN_SCALE * LOG2_E)).astype(BF16)

    ckvn = _rms_norm(ckv, kvg_ref[...]).astype(BF16)
    k_rope = zkr * c2 + zsw * s2
    k = _dot(ckvn, wk_ref[...]) + jnp.tile(k_rope, (1, A_HEADS))
    k_ref[0] = k.astype(BF16)
    vt_ref[0] = _dot_nt(wvt_ref[...], ckvn).astype(BF16)

    gz = 0.5 * zb * (1.0 + lax.erf(zb * math.sqrt(0.5)))
    u, v = gz[:, :B_WIDTH], gz[:, B_WIDTH:]
    vn = _layer_norm(v, sg_ref[...], sb_ref[...]).astype(BF16)
    gch = B_WIDTH // B_GROUPS
    for n in range(tm // CHUNK):
        rows = slice(n * CHUNK, (n + 1) * CHUNK)
        for g in range(B_GROUPS):
            cols = slice(g * gch, (g + 1) * gch)
            mixed = _dot(ws_ref[g], vn[rows, cols]) + bs_ref[g]
            bg_ref[0, rows, cols] = (u[rows, cols] * mixed).astype(BF16)


def _proj(x, mods, mod_row, ew, c2, s2, tm=256):
    bsz, length, d = x.shape
    tm = _row_tile(length, tm)
    width = A_HEADS * HEAD_PAD
    row = lambda bi, i: (bi, i, 0)
    return pl.pallas_call(
        _proj_kernel,
        grid=(bsz, length // tm),
        in_specs=[pl.BlockSpec((1, tm, d), row),
                  pl.BlockSpec((1, N_MOD, d), lambda bi, i: (mod_row(bi), 0, 0)),
                  _const_spec(ew["w_in"].shape), _const_spec((1, Q_RANK)), _const_spec((1, KV_RANK)),
                  _const_spec(ew["wq"].shape), _const_spec(ew["wk"].shape), _const_spec(ew["wvt"].shape),
                  pl.BlockSpec((tm, HEAD_PAD), lambda bi, i: (i, 0)),
                  pl.BlockSpec((tm, HEAD_PAD), lambda bi, i: (i, 0)),
                  _const_spec((1, B_WIDTH)), _const_spec((1, B_WIDTH)),
                  _const_spec(ew["w_s"].shape), _const_spec(ew["b_s"].shape)],
        out_specs=[pl.BlockSpec((1, tm, width), row),
                   pl.BlockSpec((1, tm, width), row),
                   pl.BlockSpec((1, A_HEADS * V_DIM, tm), lambda bi, i: (bi, 0, i)),
                   pl.BlockSpec((1, tm, B_WIDTH), row)],
        out_shape=[jax.ShapeDtypeStruct((bsz, length, width), BF16),
                   jax.ShapeDtypeStruct((bsz, length, width), BF16),
                   jax.ShapeDtypeStruct((bsz, A_HEADS * V_DIM, length), BF16),
                   jax.ShapeDtypeStruct((bsz, length, B_WIDTH), BF16)],
        compiler_params=_params(2),
        name="proj_even",
    )(x, mods, ew["w_in"], ew["q_norm"], ew["kv_norm"], ew["wq"], ew["wk"], ew["wvt"], c2, s2,
      ew["sgu_g"], ew["sgu_b"], ew["w_s"], ew["b_s"])


def _attn_kernel(q_ref, k_ref, vt_ref, o_ref, *, ck):
    q = q_ref[0]
    lk = k_ref.shape[1]
    ones = jnp.ones((BF16_ROWS, ck), BF16)
    n_chunks = lk // ck
    scores = lambda c: _dot_nt(k_ref[0, c * ck:(c + 1) * ck, :], q)
    m = acc = None
    s_next = scores(0)
    for c in range(n_chunks):
        s = s_next
        if c + 1 < n_chunks:
            s_next = scores(c + 1)
        m_c = jnp.max(s, axis=0, keepdims=True)
        m_new = m_c if m is None else jnp.maximum(m, m_c)
        p = jnp.exp2(s - m_new).astype(BF16)
        v_aug = jnp.concatenate([vt_ref[0, :, c * ck:(c + 1) * ck], ones], axis=0)
        pv = _dot(v_aug, p)
        acc = pv if m is None else acc * jnp.exp2(m - m_new) + pv
        m = m_new
    o_ref[0] = acc[:V_DIM] / acc[V_DIM:V_DIM + 1]


def _key_chunk(lk):
    for ck in (768, 512, 256):
        if lk % ck == 0:
            return ck
    return lk


def _attention(q, k, vt, tq=1024, ck=None):
    bsz, lq, _ = q.shape
    lk = k.shape[1]
    tq = _row_tile(lq, tq)
    return pl.pallas_call(
        functools.partial(_attn_kernel, ck=ck or _key_chunk(lk)),
        grid=(bsz, A_HEADS, lq // tq),
        in_specs=[pl.BlockSpec((1, tq, HEAD_PAD), lambda bi, h, i: (bi, i, h)),
                  pl.BlockSpec((1, lk, HEAD_PAD), lambda bi, h, i: (bi, 0, h)),
                  pl.BlockSpec((1, V_DIM, lk), lambda bi, h, i: (bi, h, 0))],
        out_specs=pl.BlockSpec((1, V_DIM, tq), lambda bi, h, i: (bi, h, i)),
        out_shape=jax.ShapeDtypeStruct((bsz, A_HEADS * V_DIM, lq), F32),
        compiler_params=_params(3),
        name="attention",
    )(q, k, vt)


def _outres_kernel(x_ref, mod_ref, at_ref, bg_ref, woa_ref, wob_ref, g_ref, b_ref, o_ref, *, alpha):
    a = at_ref[0].T.astype(BF16)
    y = _dot(a, woa_ref[...]) + _dot(bg_ref[0], wob_ref[...])
    r = alpha * x_ref[0] + mod_ref[0][5:6] * y
    o_ref[0] = _layer_norm(r, g_ref[...], b_ref[...])


def _outres(x, mods, mod_row, at, bg, woa, wob, g, b, alpha, tm=512):
    bsz, length, d = x.shape
    tm = _row_tile(length, tm)
    row = lambda bi, i: (bi, i, 0)
    return pl.pallas_call(
        functools.partial(_outres_kernel, alpha=alpha),
        grid=(bsz, length // tm),
        in_specs=[pl.BlockSpec((1, tm, d), row),
                  pl.BlockSpec((1, N_MOD, d), lambda bi, i: (mod_row(bi), 0, 0)),
                  pl.BlockSpec((1, A_HEADS * V_DIM, tm), lambda bi, i: (bi, 0, i)),
                  pl.BlockSpec((1, tm, B_WIDTH), row),
                  _const_spec(woa.shape), _const_spec(wob.shape),
                  _const_spec((1, d)), _const_spec((1, d))],
        out_specs=pl.BlockSpec((1, tm, d), row),
        out_shape=jax.ShapeDtypeStruct(x.shape, F32),
        compiler_params=_params(2),
        name="outres_even",
    )(x, mods, at, bg, woa, wob, g.reshape(1, d), b.reshape(1, d))


def _conv_kernel(x_ref, xp_ref, xn_ref, mod_ref, wpw_ref, bpw_ref, wdw_ref, bdw_ref, cg_ref, cb_ref,
                 wo_ref, bo_ref, g_ref, b_ref, o_ref, y_ref, c_ref, *, alpha):
    i = pl.program_id(1)
    last = pl.num_programs(1) - 1
    x = x_ref[0]
    tm, d = x.shape
    m = mod_ref[0]
    xcat = jnp.concatenate([xp_ref[0], x, xn_ref[0]], axis=0)
    h = _modulate(xcat, m, 1).astype(BF16)
    z = _dot(h, wpw_ref[...]) + bpw_ref[...]
    y = z[:, :d] * jax.nn.sigmoid(z[:, d:])
    r = lax.broadcasted_iota(jnp.int32, (tm + 2 * CONV_HALO, 1), 0)
    pad = ((r < CONV_HALO) & (i == 0)) | ((r >= tm + CONV_HALO) & (i == last))
    y_ref[...] = jnp.where(pad, 0.0, y)

    off = CONV_HALO - CONV_W // 2
    for rb in range(tm // CONV_ROWS):
        r0 = rb * CONV_ROWS
        for lb in range(d // LANES):
            lanes = slice(lb * LANES, (lb + 1) * LANES)
            acc = jnp.zeros((CONV_ROWS, LANES), F32) + bdw_ref[:, lanes]
            for res in range(SUBLANES):
                part = None
                for a in range((CONV_W + off) // SUBLANES + 1):
                    t = a * SUBLANES + res - off
                    if 0 <= t < CONV_W:
                        rows = slice(r0 + a * SUBLANES, r0 + a * SUBLANES + CONV_ROWS + SUBLANES)
                        term = wdw_ref[t:t + 1, lanes] * y_ref[rows, lanes]
                        part = term if part is None else part + term
                acc = acc + part[res:res + CONV_ROWS]
            c_ref[r0:r0 + CONV_ROWS, lanes] = acc

    t2 = jax.nn.silu(_layer_norm(c_ref[...], cg_ref[...], cb_ref[...])).astype(BF16)
    y2 = _dot(t2, wo_ref[...]) + bo_ref[...]
    res = alpha * x + m[5:6] * y2
    o_ref[0] = _layer_norm(res, g_ref[...], b_ref[...])


def _convmix(x, mods, mod_row, ow, g, b, alpha, tm=512):
    bsz, length, d = x.shape
    tm = _row_tile(length, tm)
    per = tm // CONV_HALO
    n_halo = length // CONV_HALO
    row = lambda bi, i: (bi, i, 0)
    return pl.pallas_call(
        functools.partial(_conv_kernel, alpha=alpha),
        grid=(bsz, length // tm),
        in_specs=[pl.BlockSpec((1, tm, d), row),
                  pl.BlockSpec((1, CONV_HALO, d), lambda bi, i: (bi, jnp.maximum(i * per - 1, 0), 0)),
                  pl.BlockSpec((1, CONV_HALO, d),
                               lambda bi, i: (bi, jnp.minimum((i + 1) * per, n_halo - 1), 0)),
                  pl.BlockSpec((1, N_MOD, d), lambda bi, i: (mod_row(bi), 0, 0)),
                  _const_spec(ow["w_pw1"].shape), _const_spec((1, 2 * d)),
                  _const_spec((CONV_W, d)), _const_spec((1, d)), _const_spec((1, d)), _const_spec((1, d)),
                  _const_spec(ow["w_out"].shape), _const_spec((1, d)),
                  _const_spec((1, d)), _const_spec((1, d))],
        out_specs=pl.BlockSpec((1, tm, d), row),
        out_shape=jax.ShapeDtypeStruct(x.shape, F32),
        scratch_shapes=[pltpu.VMEM((tm + 2 * CONV_HALO, d), F32), pltpu.VMEM((tm, d), F32)],
        compiler_params=_params(2),
        name="conv_odd",
    )(x, x, x, mods, ow["w_pw1"], ow["b_pw1"], ow["w_dw"], ow["b_dw"], ow["ln_g"], ow["ln_b"],
      ow["w_out"], ow["b_out"], g.reshape(1, d), b.reshape(1, d))


def _pad_cols(w, left, total):
    return jnp.pad(w, ((0, 0), (left, total - left - w.shape[1])))


def _even_weights(w_in, q_norm, kv_norm, w_q_up, w_kv_up, sgu_g, sgu_b, w_s, b_s, w_out):
    half = QK_ROPE // 2
    perm = jnp.concatenate([jnp.arange(0, QK_ROPE, 2), jnp.arange(1, QK_ROPE, 2)])
    swap = jnp.concatenate([perm[half:], perm[:half]])
    o_kv = Q_RANK
    o_kr = o_kv + KV_RANK
    o_b = o_kr + QK_ROPE
    kr = w_in[:, o_kr:o_b]
    w_in_ext = jnp.concatenate([w_in[:, :o_kr],
                                _pad_cols(kr[:, perm], QK_NOPE, HEAD_PAD),
                                _pad_cols(kr[:, swap], QK_NOPE, HEAD_PAD),
                                w_in[:, o_b:]], axis=1)
    wq = w_q_up.reshape(Q_RANK, A_HEADS, QK_NOPE + QK_ROPE)
    nope, rope = wq[:, :, :QK_NOPE], wq[:, :, QK_NOPE:]
    zeros_tail = jnp.zeros((Q_RANK, A_HEADS, HEAD_PAD - QK_NOPE - QK_ROPE), w_q_up.dtype)
    wq_a = jnp.concatenate([nope, rope[:, :, perm], zeros_tail], axis=2)
    wq_b = jnp.concatenate([jnp.zeros_like(nope), rope[:, :, swap], zeros_tail], axis=2)
    wq_ab = jnp.concatenate([wq_a.reshape(Q_RANK, -1), wq_b.reshape(Q_RANK, -1)], axis=1)
    wkv = w_kv_up.reshape(KV_RANK, A_HEADS, QK_NOPE + V_DIM)
    wk = jnp.concatenate([wkv[:, :, :QK_NOPE],
                          jnp.zeros((KV_RANK, A_HEADS, HEAD_PAD - QK_NOPE), w_kv_up.dtype)], axis=2)
    wvt = wkv[:, :, QK_NOPE:].reshape(KV_RANK, A_HEADS * V_DIM).T
    a_width = A_HEADS * V_DIM
    return {
        "w_in": w_in_ext.astype(BF16),
        "q_norm": q_norm.reshape(1, Q_RANK), "kv_norm": kv_norm.reshape(1, KV_RANK),
        "wq": wq_ab.astype(BF16), "wk": wk.reshape(KV_RANK, -1).astype(BF16), "wvt": wvt.astype(BF16),
        "sgu_g": sgu_g.reshape(1, B_WIDTH), "sgu_b": sgu_b.reshape(1, B_WIDTH),
        "w_s": w_s.astype(BF16),
        "b_s": jnp.broadcast_to(b_s[:, :, None], (B_GROUPS, CHUNK, B_WIDTH // B_GROUPS)),
        "wo_a": w_out[:a_width].astype(BF16), "wo_b": w_out[a_width:].astype(BF16),
    }


def _rope_tables(length):
    half = QK_ROPE // 2
    rows = jnp.repeat(jnp.arange(length // GRID_W, dtype=F32), GRID_W)
    cols = jnp.tile(jnp.arange(GRID_W, dtype=F32), length // GRID_W)
    inv = 1.0 / (ROPE_BASE ** (jnp.arange(0, half, 2, dtype=F32) / half))
    ang = jnp.concatenate([rows[:, None] * inv, cols[:, None] * inv], axis=-1)
    cos, sin = jnp.cos(ang), jnp.sin(ang)
    tail = jnp.zeros((length, HEAD_PAD - QK_NOPE - QK_ROPE), F32)
    c2 = jnp.concatenate([jnp.ones((length, QK_NOPE), F32), cos, cos, tail], axis=1)
    s2 = jnp.concatenate([jnp.zeros((length, QK_NOPE), F32), -sin, sin, tail], axis=1)
    return c2, s2


def _identity_tables(length):
    c2 = jnp.concatenate([jnp.ones((length, QK_NOPE + QK_ROPE), F32),
                          jnp.zeros((length, HEAD_PAD - QK_NOPE - QK_ROPE), F32)], axis=1)
    return c2, jnp.zeros((length, HEAD_PAD), F32)


def _ctx_needed(l, depth):
    return any(j % 2 == 0 for j in range(l, depth))


def kernel(x, c, ctx, c_ctx, w_mod, b_mod, ln_g, ln_b, ffn_w13, ffn_w2, e_w_in, e_q_norm, e_kv_norm, e_w_q_up, e_w_kv_up, e_sgu_g, e_sgu_b, e_w_s, e_b_s, e_w_out, o_w_pw1, o_b_pw1, o_w_dw, o_b_dw, o_ln_g, o_ln_b, o_w_out, o_b_out):
    bsz, length, d = x.shape
    ctx_len = ctx.shape[1]
    depth = w_mod.shape[0]
    alpha = (2 * depth) ** 0.25

    n_rows = -(-(bsz + 1) // SUBLANES) * SUBLANES
    conds = jnp.zeros((n_rows, d), F32).at[:bsz].set(c).at[bsz].set(c_ctx)
    mods = _adaln(conds, w_mod, b_mod).reshape(depth, n_rows, N_MOD, d)
    lat_row = lambda bi: bi
    ctx_row = lambda bi: bsz

    w13 = ffn_w13.astype(BF16)
    w2 = ffn_w2.astype(BF16)
    c2_lat, s2_lat = _rope_tables(length)
    c2_ctx, s2_ctx = _identity_tables(ctx_len)

    lat, cx = x, ctx
    for l in range(depth):
        ctx_in = _ctx_needed(l, depth)
        ctx_out = _ctx_needed(l + 1, depth)
        ml = mods[l]
        ffn_a = (w13[l, 0], w2[l, 0], ln_g[l, 0], ln_b[l, 0], alpha)
        ffn_b = (w13[l, 1], w2[l, 1], ln_g[l, 2], ln_b[l, 2], alpha)
        lat = _ffn(lat, ml, lat_row, 0, *ffn_a)
        if ctx_in:
            cx = _ffn(cx, ml, ctx_row, 0, *ffn_a)
        if l % 2 == 0:
            e = l // 2
            ew = _even_weights(e_w_in[e], e_q_norm[e], e_kv_norm[e], e_w_q_up[e], e_w_kv_up[e],
                               e_sgu_g[e], e_sgu_b[e], e_w_s[e], e_b_s[e], e_w_out[e])
            q_l, k_l, vt_l, bg_l = _proj(lat, ml, lat_row, ew, c2_lat, s2_lat)
            q_c, k_c, vt_c, bg_c = _proj(cx, ml, ctx_row, ew, c2_ctx, s2_ctx)
            k_all = jnp.concatenate([k_c, k_l], axis=1)
            vt_all = jnp.concatenate([vt_c, vt_l], axis=2)
            at_l = _attention(q_l, k_all, vt_all)
            res = (ew["wo_a"], ew["wo_b"], ln_g[l, 1], ln_b[l, 1], alpha)
            lat = _outres(lat, ml, lat_row, at_l, bg_l, *res)
            if ctx_out:
                at_c = _attention(q_c, k_c, vt_c)
                cx = _outres(cx, ml, ctx_row, at_c, bg_c, *res)
        else:
            o = l // 2
            ow = {"w_pw1": o_w_pw1[o].astype(BF16), "b_pw1": o_b_pw1[o].reshape(1, -1),
                  "w_dw": o_w_dw[o], "b_dw": o_b_dw[o].reshape(1, d),
                  "ln_g": o_ln_g[o].reshape(1, d), "ln_b": o_ln_b[o].reshape(1, d),
                  "w_out": o_w_out[o].astype(BF16), "b_out": o_b_out[o].reshape(1, d)}
            lat = _convmix(lat, ml, lat_row, ow, ln_g[l, 1], ln_b[l, 1], alpha)
            if ctx_out:
                cx = _convmix(cx, ml, ctx_row, ow, ln_g[l, 1], ln_b[l, 1], alpha)
        lat = _ffn(lat, ml, lat_row, 2, *ffn_b)
        if ctx_out:
            cx = _ffn(cx, ml, ctx_row, 2, *ffn_b)
    return lat
```

```python
import functools
import math

import jax
import jax.numpy as jnp
from jax import lax
from jax.experimental import pallas as pl
from jax.experimental.pallas import tpu as pltpu

N_MOD = 9
D_FF = 2816
MACARON = 0.5
A_HEADS = 8
QK_NOPE = 64
QK_ROPE = 32
V_DIM = 64
Q_RANK = 384
KV_RANK = 256
ATTN_SCALE = 1.0 / math.sqrt(QK_NOPE + QK_ROPE)
ROPE_BASE = 10000.0
GRID_W = 64
CHUNK = 128
B_GROUPS = 4
B_WIDTH = 512
CONV_W = 31
LN_EPS = 1e-5
RMS_EPS = 1e-6

LANES = 128
SUBLANES = 8
BF16_ROWS = 16
ATTN_KEY_CHUNK = 256
ATTN_QGROUP = 256
ATTN_LOOKAHEAD = 2
LOG2_E = math.log2(math.e)
HEAD_PAD = 128
V7X_VMEM_LIMIT = 56 * 1024 * 1024

CONV_HALO = 16
CONV_ROWS = 128

BF16 = jnp.bfloat16
F32 = jnp.float32


def _dot(a, b):
    return jnp.dot(a, b, preferred_element_type=F32)


def _dot_nt(a, b):
    return lax.dot_general(a, b, (((1,), (1,)), ((), ())), preferred_element_type=F32)


def _layer_norm(x, g, b):
    mu = jnp.mean(x, axis=-1, keepdims=True)
    xc = x - mu
    var = jnp.mean(xc * xc, axis=-1, keepdims=True)
    return xc * lax.rsqrt(var + LN_EPS) * g + b


def _rms_norm(x, g):
    return x * lax.rsqrt(jnp.mean(x * x, axis=-1, keepdims=True) + RMS_EPS) * g


def _modulate(x, m, idx):
    return x * (1.0 + m[3 * idx + 1:3 * idx + 2]) + m[3 * idx:3 * idx + 1]


def _params(n_grid):
    return pltpu.CompilerParams(dimension_semantics=("arbitrary",) * n_grid,
                                vmem_limit_bytes=V7X_VMEM_LIMIT)


def _const_spec(shape):
    return pl.BlockSpec(shape, lambda *_: (0,) * len(shape), pipeline_mode=pl.Buffered(1))


def _row_tile(length, want):
    return min(want, length)


def _adaln_kernel(c_ref, w_ref, b_ref, o_ref):
    a = jax.nn.silu(c_ref[...]).astype(BF16)
    o_ref[0] = _dot(a, w_ref[0].astype(BF16)) + b_ref[0]


def _adaln(conds, w_mod, b_mod, tn=1152):
    depth, d, n = w_mod.shape
    rows = conds.shape[0]
    return pl.pallas_call(
        _adaln_kernel,
        grid=(depth, n // tn),
        in_specs=[pl.BlockSpec((rows, d), lambda l, j: (0, 0)),
                  pl.BlockSpec((1, d, tn), lambda l, j: (l, 0, j)),
                  pl.BlockSpec((1, 1, tn), lambda l, j: (l, 0, j))],
        out_specs=pl.BlockSpec((1, rows, tn), lambda l, j: (l, 0, j)),
        out_shape=jax.ShapeDtypeStruct((depth, rows, n), F32),
        compiler_params=_params(2),
        name="adaln",
    )(conds, w_mod, b_mod.reshape(depth, 1, n))


def _ffn_kernel(x_ref, mod_ref, w13_ref, w2_ref, g_ref, b_ref, o_ref, *, idx, alpha, chunk):
    x = x_ref[0]
    m = mod_ref[0]
    xm = _modulate(x, m, idx).astype(BF16)
    acc = jnp.zeros(x.shape, F32)
    for c in range(D_FF // chunk):
        gt = _dot(xm, w13_ref[:, c * chunk:(c + 1) * chunk])
        up = _dot(xm, w13_ref[:, D_FF + c * chunk:D_FF + (c + 1) * chunk])
        h = (gt * jax.nn.sigmoid(gt) * up).astype(BF16)
        acc = acc + _dot(h, w2_ref[c * chunk:(c + 1) * chunk, :])
    r = alpha * x + (MACARON * m[3 * idx + 2:3 * idx + 3]) * acc
    o_ref[0] = _layer_norm(r, g_ref[...], b_ref[...])


def _ffn(x, mods, mod_row, idx, w13, w2, g, b, alpha, tm=512, chunk=256):
    bsz, length, d = x.shape
    tm = _row_tile(length, tm)
    kernel = functools.partial(_ffn_kernel, idx=idx, alpha=alpha, chunk=chunk)
    return pl.pallas_call(
        kernel,
        grid=(bsz, length // tm),
        in_specs=[pl.BlockSpec((1, tm, d), lambda bi, i: (bi, i, 0)),
                  pl.BlockSpec((1, N_MOD, d), lambda bi, i: (mod_row(bi), 0, 0)),
                  _const_spec(w13.shape), _const_spec(w2.shape),
                  _const_spec((1, d)), _const_spec((1, d))],
        out_specs=pl.BlockSpec((1, tm, d), lambda bi, i: (bi, i, 0)),
        out_shape=jax.ShapeDtypeStruct(x.shape, F32),
        compiler_params=_params(2),
        name="ffn",
    )(x, mods, w13, w2, g.reshape(1, d), b.reshape(1, d))


def _proj_kernel(x_ref, mod_ref, win_ref, qg_ref, kvg_ref, wq_ref, wk_ref, wvt_ref, c2_ref, s2_ref,
                 sg_ref, sb_ref, ws_ref, bs_ref, q_ref, k_ref, vt_ref, bg_ref):
    x = x_ref[0]
    tm = x.shape[0]
    h = _modulate(x, mod_ref[0], 1).astype(BF16)
    z = _dot(h, win_ref[...])
    o_kv = Q_RANK
    o_kr = o_kv + KV_RANK
    o_sw = o_kr + HEAD_PAD
    o_b = o_sw + HEAD_PAD
    cq, ckv = z[:, :o_kv], z[:, o_kv:o_kr]
    zkr, zsw, zb = z[:, o_kr:o_sw], z[:, o_sw:o_b], z[:, o_b:]

    c2 = c2_ref[...]
    s2 = s2_ref[...]
    width = A_HEADS * HEAD_PAD
    cqn = _rms_norm(cq, qg_ref[...]).astype(BF16)
    qab = _dot(cqn, wq_ref[...])
    q = qab[:, :width] * jnp.tile(c2, (1, A_HEADS)) + qab[:, width:] * jnp.tile(s2, (1, A_HEADS))
    q_ref[0] = (q * (ATTN_SCALE * LOG2_E)).astype(BF16)

    ckvn = _rms_norm(ckv, kvg_ref[...]).astype(BF16)
    k_rope = zkr * c2 + zsw * s2
    k = _dot(ckvn, wk_ref[...]) + jnp.tile(k_rope, (1, A_HEADS))
    k_ref[0] = k.astype(BF16)
    vt_ref[0] = _dot_nt(wvt_ref[...], ckvn).astype(BF16)

    gz = 0.5 * zb * (1.0 + lax.erf(zb * math.sqrt(0.5)))
    u, v = gz[:, :B_WIDTH], gz[:, B_WIDTH:]
    vn = _layer_norm(v, sg_ref[...], sb_ref[...]).astype(BF16)
    gch = B_WIDTH // B_GROUPS
    for n in range(tm // CHUNK):
        rows = slice(n * CHUNK, (n + 1) * CHUNK)
        for g in range(B_GROUPS):
            cols = slice(g * gch, (g + 1) * gch)
            mixed = _dot(ws_ref[g], vn[rows, cols]) + bs_ref[g]
            bg_ref[0, rows, cols] = (u[rows, cols] * mixed).astype(BF16)


def _proj_alias_kernel(*refs):
    _proj_kernel(*refs[:14], *refs[16:])


def _proj(x, mods, mod_row, ew, c2, s2, kv_len, kv_off, kv_bufs=None, tm=512):
    bsz, length, d = x.shape
    tm = _row_tile(length, tm)
    width = A_HEADS * HEAD_PAD
    row = lambda bi, i: (bi, i, 0)
    kv_blk = kv_off // tm
    assert kv_off % tm == 0
    extra_specs, extra_args, aliases = [], (), {}
    if kv_bufs is not None:
        extra_specs = [pl.BlockSpec(memory_space=pl.ANY)] * 2
        extra_args = tuple(kv_bufs)
        aliases = {14: 1, 15: 2}
    return pl.pallas_call(
        _proj_kernel if kv_bufs is None else _proj_alias_kernel,
        grid=(bsz, length // tm),
        input_output_aliases=aliases,
        in_specs=[pl.BlockSpec((1, tm, d), row),
                  pl.BlockSpec((1, N_MOD, d), lambda bi, i: (mod_row(bi), 0, 0)),
                  _const_spec(ew["w_in"].shape), _const_spec((1, Q_RANK)), _const_spec((1, KV_RANK)),
                  _const_spec(ew["wq"].shape), _const_spec(ew["wk"].shape), _const_spec(ew["wvt"].shape),
                  pl.BlockSpec((tm, HEAD_PAD), lambda bi, i: (i, 0)),
                  pl.BlockSpec((tm, HEAD_PAD), lambda bi, i: (i, 0)),
                  _const_spec((1, B_WIDTH)), _const_spec((1, B_WIDTH)),
                  _const_spec(ew["w_s"].shape), _const_spec(ew["b_s"].shape)] + extra_specs,
        out_specs=[pl.BlockSpec((1, tm, width), row),
                   pl.BlockSpec((1, tm, width), lambda bi, i: (bi, kv_blk + i, 0)),
                   pl.BlockSpec((1, A_HEADS * V_DIM, tm), lambda bi, i: (bi, 0, kv_blk + i)),
                   pl.BlockSpec((1, tm, B_WIDTH), row)],
        out_shape=[jax.ShapeDtypeStruct((bsz, length, width), BF16),
                   jax.ShapeDtypeStruct((bsz, kv_len, width), BF16),
                   jax.ShapeDtypeStruct((bsz, A_HEADS * V_DIM, kv_len), BF16),
                   jax.ShapeDtypeStruct((bsz, length, B_WIDTH), BF16)],
        compiler_params=_params(2),
        name="proj_even",
    )(x, mods, ew["w_in"], ew["q_norm"], ew["kv_norm"], ew["wq"], ew["wk"], ew["wvt"], c2, s2,
      ew["sgu_g"], ew["sgu_b"], ew["w_s"], ew["b_s"], *extra_args)


def _attn_kernel(q_ref, k_ref, vt_ref, o_ref):
    lk = k_ref.shape[1]
    tq = q_ref.shape[1]
    gw = min(ATTN_QGROUP, tq)
    n_groups = tq // gw
    ck = ATTN_KEY_CHUNK
    n_chunks = lk // ck
    ones = jnp.ones((BF16_ROWS, ck), BF16)
    qs = [q_ref[0, g * gw:(g + 1) * gw, :] for g in range(n_groups)]
    scores = lambda c, g: _dot_nt(k_ref[0, c * ck:(c + 1) * ck, :], qs[g])
    pending = [[scores(c, g) for c in range(min(ATTN_LOOKAHEAD, n_chunks))] for g in range(n_groups)]
    m = [None] * n_groups
    acc = [None] * n_groups
    for c in range(n_chunks):
        v_aug = jnp.concatenate([vt_ref[0, :, c * ck:(c + 1) * ck], ones], axis=0)
        for g in range(n_groups):
            s = pending[g].pop(0)
            if c + ATTN_LOOKAHEAD < n_chunks:
                pending[g].append(scores(c + ATTN_LOOKAHEAD, g))
            m_c = jnp.max(s, axis=0, keepdims=True)
            m_new = m_c if m[g] is None else jnp.maximum(m[g], m_c)
            p = jnp.exp2(s - m_new).astype(BF16)
            pv = _dot(v_aug, p)
            acc[g] = pv if m[g] is None else acc[g] * jnp.exp2(m[g] - m_new) + pv
            m[g] = m_new
    for g in range(n_groups):
        o_ref[0, :, g * gw:(g + 1) * gw] = acc[g][:V_DIM] / acc[g][V_DIM:V_DIM + 1]


def _attention(q, k, vt, key_off, key_len, tq=1024):
    bsz, lq, _ = q.shape
    tq = _row_tile(lq, tq)
    assert key_len % ATTN_KEY_CHUNK == 0 and key_off % key_len == 0
    key_blk = key_off // key_len
    return pl.pallas_call(
        _attn_kernel,
        grid=(bsz, A_HEADS, lq // tq),
        in_specs=[pl.BlockSpec((1, tq, HEAD_PAD), lambda bi, h, i: (bi, i, h)),
                  pl.BlockSpec((1, key_len, HEAD_PAD), lambda bi, h, i: (bi, key_blk, h)),
                  pl.BlockSpec((1, V_DIM, key_len), lambda bi, h, i: (bi, h, key_blk))],
        out_specs=pl.BlockSpec((1, V_DIM, tq), lambda bi, h, i: (bi, h, i)),
        out_shape=jax.ShapeDtypeStruct((bsz, A_HEADS * V_DIM, lq), F32),
        compiler_params=_params(3),
        name="attention",
    )(q, k, vt)


def _outres_kernel(x_ref, mod_ref, at_ref, bg_ref, woa_ref, wob_ref, g_ref, b_ref, o_ref, *, alpha):
    a = at_ref[0].T.astype(BF16)
    y = _dot(a, woa_ref[...]) + _dot(bg_ref[0], wob_ref[...])
    r = alpha * x_ref[0] + mod_ref[0][5:6] * y
    o_ref[0] = _layer_norm(r, g_ref[...], b_ref[...])


def _outres(x, mods, mod_row, at, bg, woa, wob, g, b, alpha, tm=512):
    bsz, length, d = x.shape
    tm = _row_tile(length, tm)
    row = lambda bi, i: (bi, i, 0)
    return pl.pallas_call(
        functools.partial(_outres_kernel, alpha=alpha),
        grid=(bsz, length // tm),
        in_specs=[pl.BlockSpec((1, tm, d), row),
                  pl.BlockSpec((1, N_MOD, d), lambda bi, i: (mod_row(bi), 0, 0)),
                  pl.BlockSpec((1, A_HEADS * V_DIM, tm), lambda bi, i: (bi, 0, i)),
                  pl.BlockSpec((1, tm, B_WIDTH), row),
                  _const_spec(woa.shape), _const_spec(wob.shape),
                  _const_spec((1, d)), _const_spec((1, d))],
        out_specs=pl.BlockSpec((1, tm, d), row),
        out_shape=jax.ShapeDtypeStruct(x.shape, F32),
        compiler_params=_params(2),
        name="outres_even",
    )(x, mods, at, bg, woa, wob, g.reshape(1, d), b.reshape(1, d))


def _conv_kernel(x_ref, xp_ref, xn_ref, mod_ref, wpw_ref, bpw_ref, wdw_ref, bdw_ref, cg_ref, cb_ref,
                 wo_ref, bo_ref, g_ref, b_ref, o_ref, y_ref, c_ref, *, alpha):
    i = pl.program_id(1)
    last = pl.num_programs(1) - 1
    x = x_ref[0]
    tm, d = x.shape
    m = mod_ref[0]
    xcat = jnp.concatenate([xp_ref[0], x, xn_ref[0]], axis=0)
    h = _modulate(xcat, m, 1).astype(BF16)
    z = _dot(h, wpw_ref[...]) + bpw_ref[...]
    y = z[:, :d] * jax.nn.sigmoid(z[:, d:])
    r = lax.broadcasted_iota(jnp.int32, (tm + 2 * CONV_HALO, 1), 0)
    pad = ((r < CONV_HALO) & (i == 0)) | ((r >= tm + CONV_HALO) & (i == last))
    y_ref[...] = jnp.where(pad, 0.0, y)

    off = CONV_HALO - CONV_W // 2
    for rb in range(tm // CONV_ROWS):
        r0 = rb * CONV_ROWS
        for lb in range(d // LANES):
            lanes = slice(lb * LANES, (lb + 1) * LANES)
            acc = jnp.zeros((CONV_ROWS, LANES), F32) + bdw_ref[:, lanes]
            for res in range(SUBLANES):
                part = None
                for a in range((CONV_W + off) // SUBLANES + 1):
                    t = a * SUBLANES + res - off
                    if 0 <= t < CONV_W:
                        rows = slice(r0 + a * SUBLANES, r0 + a * SUBLANES + CONV_ROWS + SUBLANES)
                        term = wdw_ref[t:t + 1, lanes] * y_ref[rows, lanes]
                        part = term if part is None else part + term
                acc = acc + part[res:res + CONV_ROWS]
            c_ref[r0:r0 + CONV_ROWS, lanes] = acc

    t2 = jax.nn.silu(_layer_norm(c_ref[...], cg_ref[...], cb_ref[...])).astype(BF16)
    y2 = _dot(t2, wo_ref[...]) + bo_ref[...]
    res = alpha * x + m[5:6] * y2
    o_ref[0] = _layer_norm(res, g_ref[...], b_ref[...])


def _convmix(x, mods, mod_row, ow, g, b, alpha, tm=512):
    bsz, length, d = x.shape
    tm = _row_tile(length, tm)
    per = tm // CONV_HALO
    n_halo = length // CONV_HALO
    row = lambda bi, i: (bi, i, 0)
    return pl.pallas_call(
        functools.partial(_conv_kernel, alpha=alpha),
        grid=(bsz, length // tm),
        in_specs=[pl.BlockSpec((1, tm, d), row),
                  pl.BlockSpec((1, CONV_HALO, d), lambda bi, i: (bi, jnp.maximum(i * per - 1, 0), 0)),
                  pl.BlockSpec((1, CONV_HALO, d),
                               lambda bi, i: (bi, jnp.minimum((i + 1) * per, n_halo - 1), 0)),
                  pl.BlockSpec((1, N_MOD, d), lambda bi, i: (mod_row(bi), 0, 0)),
                  _const_spec(ow["w_pw1"].shape), _const_spec((1, 2 * d)),
                  _const_spec((CONV_W, d)), _const_spec((1, d)), _const_spec((1, d)), _const_spec((1, d)),
                  _const_spec(ow["w_out"].shape), _const_spec((1, d)),
                  _const_spec((1, d)), _const_spec((1, d))],
        out_specs=pl.BlockSpec((1, tm, d), row),
        out_shape=jax.ShapeDtypeStruct(x.shape, F32),
        scratch_shapes=[pltpu.VMEM((tm + 2 * CONV_HALO, d), F32), pltpu.VMEM((tm, d), F32)],
        compiler_params=_params(2),
        name="conv_odd",
    )(x, x, x, mods, ow["w_pw1"], ow["b_pw1"], ow["w_dw"], ow["b_dw"], ow["ln_g"], ow["ln_b"],
      ow["w_out"], ow["b_out"], g.reshape(1, d), b.reshape(1, d))


def _pad_cols(w, left, total):
    return jnp.pad(w, ((0, 0), (left, total - left - w.shape[1])))


def _even_weights(w_in, q_norm, kv_norm, w_q_up, w_kv_up, sgu_g, sgu_b, w_s, b_s, w_out):
    half = QK_ROPE // 2
    perm = jnp.concatenate([jnp.arange(0, QK_ROPE, 2), jnp.arange(1, QK_ROPE, 2)])
    swap = jnp.concatenate([perm[half:], perm[:half]])
    o_kv = Q_RANK
    o_kr = o_kv + KV_RANK
    o_b = o_kr + QK_ROPE
    kr = w_in[:, o_kr:o_b]
    w_in_ext = jnp.concatenate([w_in[:, :o_kr],
                                _pad_cols(kr[:, perm], QK_NOPE, HEAD_PAD),
                                _pad_cols(kr[:, swap], QK_NOPE, HEAD_PAD),
                                w_in[:, o_b:]], axis=1)
    wq = w_q_up.reshape(Q_RANK, A_HEADS, QK_NOPE + QK_ROPE)
    nope, rope = wq[:, :, :QK_NOPE], wq[:, :, QK_NOPE:]
    zeros_tail = jnp.zeros((Q_RANK, A_HEADS, HEAD_PAD - QK_NOPE - QK_ROPE), w_q_up.dtype)
    wq_a = jnp.concatenate([nope, rope[:, :, perm], zeros_tail], axis=2)
    wq_b = jnp.concatenate([jnp.zeros_like(nope), rope[:, :, swap], zeros_tail], axis=2)
    wq_ab = jnp.concatenate([wq_a.reshape(Q_RANK, -1), wq_b.reshape(Q_RANK, -1)], axis=1)
    wkv = w_kv_up.reshape(KV_RANK, A_HEADS, QK_NOPE + V_DIM)
    wk = jnp.concatenate([wkv[:, :, :QK_NOPE],
                          jnp.zeros((KV_RANK, A_HEADS, HEAD_PAD - QK_NOPE), w_kv_up.dtype)], axis=2)
    wvt = wkv[:, :, QK_NOPE:].reshape(KV_RANK, A_HEADS * V_DIM).T
    a_width = A_HEADS * V_DIM
    return {
        "w_in": w_in_ext.astype(BF16),
        "q_norm": q_norm.reshape(1, Q_RANK), "kv_norm": kv_norm.reshape(1, KV_RANK),
        "wq": wq_ab.astype(BF16), "wk": wk.reshape(KV_RANK, -1).astype(BF16), "wvt": wvt.astype(BF16),
        "sgu_g": sgu_g.reshape(1, B_WIDTH), "sgu_b": sgu_b.reshape(1, B_WIDTH),
        "w_s": w_s.astype(BF16),
        "b_s": jnp.broadcast_to(b_s[:, :, None], (B_GROUPS, CHUNK, B_WIDTH // B_GROUPS)),
        "wo_a": w_out[:a_width].astype(BF16), "wo_b": w_out[a_width:].astype(BF16),
    }


def _rope_tables(length):
    half = QK_ROPE // 2
    rows = jnp.repeat(jnp.arange(length // GRID_W, dtype=F32), GRID_W)
    cols = jnp.tile(jnp.arange(GRID_W, dtype=F32), length // GRID_W)
    inv = 1.0 / (ROPE_BASE ** (jnp.arange(0, half, 2, dtype=F32) / half))
    ang = jnp.concatenate([rows[:, None] * inv, cols[:, None] * inv], axis=-1)
    cos, sin = jnp.cos(ang), jnp.sin(ang)
    tail = jnp.zeros((length, HEAD_PAD - QK_NOPE - QK_ROPE), F32)
    c2 = jnp.concatenate([jnp.ones((length, QK_NOPE), F32), cos, cos, tail], axis=1)
    s2 = jnp.concatenate([jnp.zeros((length, QK_NOPE), F32), -sin, sin, tail], axis=1)
    return c2, s2


def _identity_tables(length):
    c2 = jnp.concatenate([jnp.ones((length, QK_NOPE + QK_ROPE), F32),
                          jnp.zeros((length, HEAD_PAD - QK_NOPE - QK_ROPE), F32)], axis=1)
    return c2, jnp.zeros((length, HEAD_PAD), F32)


def _ctx_needed(l, depth):
    return any(j % 2 == 0 for j in range(l, depth))


def kernel(x, c, ctx, c_ctx, w_mod, b_mod, ln_g, ln_b, ffn_w13, ffn_w2, e_w_in, e_q_norm, e_kv_norm, e_w_q_up, e_w_kv_up, e_sgu_g, e_sgu_b, e_w_s, e_b_s, e_w_out, o_w_pw1, o_b_pw1, o_w_dw, o_b_dw, o_ln_g, o_ln_b, o_w_out, o_b_out):
    bsz, length, d = x.shape
    ctx_len = ctx.shape[1]
    depth = w_mod.shape[0]
    alpha = (2 * depth) ** 0.25

    n_rows = -(-(bsz + 1) // SUBLANES) * SUBLANES
    conds = jnp.zeros((n_rows, d), F32).at[:bsz].set(c).at[bsz].set(c_ctx)
    mods = _adaln(conds, w_mod, b_mod).reshape(depth, n_rows, N_MOD, d)
    lat_row = lambda bi: bi
    ctx_row = lambda bi: bsz

    w13 = ffn_w13.astype(BF16)
    w2 = ffn_w2.astype(BF16)
    c2_lat, s2_lat = _rope_tables(length)
    c2_ctx, s2_ctx = _identity_tables(ctx_len)

    lat, cx = x, ctx
    for l in range(depth):
        ctx_in = _ctx_needed(l, depth)
        ctx_out = _ctx_needed(l + 1, depth)
        ml = mods[l]
        ffn_a = (w13[l, 0], w2[l, 0], ln_g[l, 0], ln_b[l, 0], alpha)
        ffn_b = (w13[l, 1], w2[l, 1], ln_g[l, 2], ln_b[l, 2], alpha)
        lat = _ffn(lat, ml, lat_row, 0, *ffn_a)
        if ctx_in:
            cx = _ffn(cx, ml, ctx_row, 0, *ffn_a)
        if l % 2 == 0:
            e = l // 2
            ew = _even_weights(e_w_in[e], e_q_norm[e], e_kv_norm[e], e_w_q_up[e], e_w_kv_up[e],
                               e_sgu_g[e], e_sgu_b[e], e_w_s[e], e_b_s[e], e_w_out[e])
            kv_len = length + ctx_len
            q_l, k_all, vt_all, bg_l = _proj(lat, ml, lat_row, ew, c2_lat, s2_lat, kv_len, 0)
            q_c, k_all, vt_all, bg_c = _proj(cx, ml, ctx_row, ew, c2_ctx, s2_ctx, kv_len, length,
                                             kv_bufs=(k_all, vt_all))
            at_l = _attention(q_l, k_all, vt_all, 0, kv_len)
            res = (ew["wo_a"], ew["wo_b"], ln_g[l, 1], ln_b[l, 1], alpha)
            lat = _outres(lat, ml, lat_row, at_l, bg_l, *res)
            if ctx_out:
                at_c = _attention(q_c, k_all, vt_all, length, ctx_len)
                cx = _outres(cx, ml, ctx_row, at_c, bg_c, *res)
        else:
            o = l // 2
            ow = {"w_pw1": o_w_pw1[o].astype(BF16), "b_pw1": o_b_pw1[o].reshape(1, -1),
                  "w_dw": o_w_dw[o], "b_dw": o_b_dw[o].reshape(1, d),
                  "ln_g": o_ln_g[o].reshape(1, d), "ln_b": o_ln_b[o].reshape(1, d),
                  "w_out": o_w_out[o].astype(BF16), "b_out": o_b_out[o].reshape(1, d)}
            lat = _convmix(lat, ml, lat_row, ow, ln_g[l, 1], ln_b[l, 1], alpha)
            if ctx_out:
                cx = _convmix(cx, ml, ctx_row, ow, ln_g[l, 1], ln_b[l, 1], alpha)
        lat = _ffn(lat, ml, lat_row, 2, *ffn_b)
        if ctx_out:
            cx = _ffn(cx, ml, ctx_row, 2, *ffn_b)
    return lat
```

```python
import functools
import math

import jax
import jax.numpy as jnp
from jax import lax
from jax.experimental import pallas as pl
from jax.experimental.pallas import tpu as pltpu

N_MOD = 9
D_FF = 2816
MACARON = 0.5
A_HEADS = 8
QK_NOPE = 64
QK_ROPE = 32
V_DIM = 64
Q_RANK = 384
KV_RANK = 256
ATTN_SCALE = 1.0 / math.sqrt(QK_NOPE + QK_ROPE)
ROPE_BASE = 10000.0
GRID_W = 64
CHUNK = 128
B_GROUPS = 4
B_WIDTH = 512
CONV_W = 31
LN_EPS = 1e-5
RMS_EPS = 1e-6

LANES = 128
SUBLANES = 8
BF16_ROWS = 16
ATTN_KEY_CHUNK = 256
ATTN_QGROUP = 256
ATTN_LOOKAHEAD = 2
LOG2_E = math.log2(math.e)
HEAD_PAD = 128
V7X_VMEM_LIMIT = 56 * 1024 * 1024

CONV_HALO = 16
CONV_ROWS = 128

BF16 = jnp.bfloat16
F32 = jnp.float32


def _dot(a, b):
    return jnp.dot(a, b, preferred_element_type=F32)


def _dot_nt(a, b):
    return lax.dot_general(a, b, (((1,), (1,)), ((), ())), preferred_element_type=F32)


def _layer_norm(x, g, b):
    mu = jnp.mean(x, axis=-1, keepdims=True)
    xc = x - mu
    var = jnp.mean(xc * xc, axis=-1, keepdims=True)
    return xc * lax.rsqrt(var + LN_EPS) * g + b


def _rms_norm(x, g):
    return x * lax.rsqrt(jnp.mean(x * x, axis=-1, keepdims=True) + RMS_EPS) * g


def _modulate(x, m, idx):
    return x * (1.0 + m[3 * idx + 1:3 * idx + 2]) + m[3 * idx:3 * idx + 1]


def _params(n_grid):
    return pltpu.CompilerParams(dimension_semantics=("arbitrary",) * n_grid,
                                vmem_limit_bytes=V7X_VMEM_LIMIT)


def _const_spec(shape):
    return pl.BlockSpec(shape, lambda *_: (0,) * len(shape), pipeline_mode=pl.Buffered(1))


def _row_tile(length, want):
    return min(want, length)


def _adaln_kernel(c_ref, w_ref, b_ref, o_ref):
    a = jax.nn.silu(c_ref[...]).astype(BF16)
    o_ref[0] = _dot(a, w_ref[0].astype(BF16)) + b_ref[0]


def _adaln(conds, w_mod, b_mod, tn=1152):
    depth, d, n = w_mod.shape
    rows = conds.shape[0]
    return pl.pallas_call(
        _adaln_kernel,
        grid=(depth, n // tn),
        in_specs=[pl.BlockSpec((rows, d), lambda l, j: (0, 0)),
                  pl.BlockSpec((1, d, tn), lambda l, j: (l, 0, j)),
                  pl.BlockSpec((1, 1, tn), lambda l, j: (l, 0, j))],
        out_specs=pl.BlockSpec((1, rows, tn), lambda l, j: (l, 0, j)),
        out_shape=jax.ShapeDtypeStruct((depth, rows, n), F32),
        compiler_params=_params(2),
        name="adaln",
    )(conds, w_mod, b_mod.reshape(depth, 1, n))


def _ffn_kernel(x_ref, mod_ref, w13_ref, w2_ref, g_ref, b_ref, o_ref, *, idx, alpha, chunk):
    x = x_ref[0]
    m = mod_ref[0]
    xm = _modulate(x, m, idx).astype(BF16)
    acc = jnp.zeros(x.shape, F32)
    for c in range(D_FF // chunk):
        gt = _dot(xm, w13_ref[:, c * chunk:(c + 1) * chunk])
        up = _dot(xm, w13_ref[:, D_FF + c * chunk:D_FF + (c + 1) * chunk])
        h = (gt * jax.nn.sigmoid(gt) * up).astype(BF16)
        acc = acc + _dot(h, w2_ref[c * chunk:(c + 1) * chunk, :])
    r = alpha * x + (MACARON * m[3 * idx + 2:3 * idx + 3]) * acc
    o_ref[0] = _layer_norm(r, g_ref[...], b_ref[...])


def _ffn(x, mods, mod_row, idx, w13, w2, g, b, alpha, tm=1024, chunk=256):
    bsz, length, d = x.shape
    tm = _row_tile(length, tm)
    kernel = functools.partial(_ffn_kernel, idx=idx, alpha=alpha, chunk=chunk)
    return pl.pallas_call(
        kernel,
        grid=(bsz, length // tm),
        in_specs=[pl.BlockSpec((1, tm, d), lambda bi, i: (bi, i, 0)),
                  pl.BlockSpec((1, N_MOD, d), lambda bi, i: (mod_row(bi), 0, 0)),
                  _const_spec(w13.shape), _const_spec(w2.shape),
                  _const_spec((1, d)), _const_spec((1, d))],
        out_specs=pl.BlockSpec((1, tm, d), lambda bi, i: (bi, i, 0)),
        out_shape=jax.ShapeDtypeStruct(x.shape, F32),
        compiler_params=_params(2),
        name="ffn",
    )(x, mods, w13, w2, g.reshape(1, d), b.reshape(1, d))


def _proj_kernel(x_ref, mod_ref, win_ref, qg_ref, kvg_ref, wq_ref, wk_ref, wvt_ref, c2_ref, s2_ref,
                 sg_ref, sb_ref, ws_ref, bs_ref, q_ref, k_ref, vt_ref, bg_ref):
    x = x_ref[0]
    tm = x.shape[0]
    h = _modulate(x, mod_ref[0], 1).astype(BF16)
    z = _dot(h, win_ref[...])
    o_kv = Q_RANK
    o_kr = o_kv + KV_RANK
    o_sw = o_kr + HEAD_PAD
    o_b = o_sw + HEAD_PAD
    cq, ckv = z[:, :o_kv], z[:, o_kv:o_kr]
    zkr, zsw, zb = z[:, o_kr:o_sw], z[:, o_sw:o_b], z[:, o_b:]

    c2 = c2_ref[...]
    s2 = s2_ref[...]
    width = A_HEADS * HEAD_PAD
    cqn = _rms_norm(cq, qg_ref[...]).astype(BF16)
    qab = _dot(cqn, wq_ref[...])
    q = qab[:, :width] * jnp.tile(c2, (1, A_HEADS)) + qab[:, width:] * jnp.tile(s2, (1, A_HEADS))
    q_ref[0] = (q * (ATTN_SCALE * LOG2_E)).astype(BF16)

    ckvn = _rms_norm(ckv, kvg_ref[...]).astype(BF16)
    k_rope = zkr * c2 + zsw * s2
    k = _dot(ckvn, wk_ref[...]) + jnp.tile(k_rope, (1, A_HEADS))
    k_ref[0] = k.astype(BF16)
    vt_ref[0] = _dot_nt(wvt_ref[...], ckvn).astype(BF16)

    gz = 0.5 * zb * (1.0 + lax.erf(zb * math.sqrt(0.5)))
    u, v = gz[:, :B_WIDTH], gz[:, B_WIDTH:]
    vn = _layer_norm(v, sg_ref[...], sb_ref[...]).astype(BF16)
    gch = B_WIDTH // B_GROUPS
    for n in range(tm // CHUNK):
        rows = slice(n * CHUNK, (n + 1) * CHUNK)
        for g in range(B_GROUPS):
            cols = slice(g * gch, (g + 1) * gch)
            mixed = _dot(ws_ref[g], vn[rows, cols]) + bs_ref[g]
            bg_ref[0, rows, cols] = (u[rows, cols] * mixed).astype(BF16)


def _proj_alias_kernel(*refs):
    _proj_kernel(*refs[:14], *refs[16:])


def _proj(x, mods, mod_row, ew, c2, s2, kv_len, kv_off, kv_bufs=None, tm=512):
    bsz, length, d = x.shape
    tm = _row_tile(length, tm)
    width = A_HEADS * HEAD_PAD
    row = lambda bi, i: (bi, i, 0)
    kv_blk = kv_off // tm
    assert kv_off % tm == 0
    extra_specs, extra_args, aliases = [], (), {}
    if kv_bufs is not None:
        extra_specs = [pl.BlockSpec(memory_space=pl.ANY)] * 2
        extra_args = tuple(kv_bufs)
        aliases = {14: 1, 15: 2}
    return pl.pallas_call(
        _proj_kernel if kv_bufs is None else _proj_alias_kernel,
        grid=(bsz, length // tm),
        input_output_aliases=aliases,
        in_specs=[pl.BlockSpec((1, tm, d), row),
                  pl.BlockSpec((1, N_MOD, d), lambda bi, i: (mod_row(bi), 0, 0)),
                  _const_spec(ew["w_in"].shape), _const_spec((1, Q_RANK)), _const_spec((1, KV_RANK)),
                  _const_spec(ew["wq"].shape), _const_spec(ew["wk"].shape), _const_spec(ew["wvt"].shape),
                  pl.BlockSpec((tm, HEAD_PAD), lambda bi, i: (i, 0)),
                  pl.BlockSpec((tm, HEAD_PAD), lambda bi, i: (i, 0)),
                  _const_spec((1, B_WIDTH)), _const_spec((1, B_WIDTH)),
                  _const_spec(ew["w_s"].shape), _const_spec(ew["b_s"].shape)] + extra_specs,
        out_specs=[pl.BlockSpec((1, tm, width), row),
                   pl.BlockSpec((1, tm, width), lambda bi, i: (bi, kv_blk + i, 0)),
                   pl.BlockSpec((1, A_HEADS * V_DIM, tm), lambda bi, i: (bi, 0, kv_blk + i)),
                   pl.BlockSpec((1, tm, B_WIDTH), row)],
        out_shape=[jax.ShapeDtypeStruct((bsz, length, width), BF16),
                   jax.ShapeDtypeStruct((bsz, kv_len, width), BF16),
                   jax.ShapeDtypeStruct((bsz, A_HEADS * V_DIM, kv_len), BF16),
                   jax.ShapeDtypeStruct((bsz, length, B_WIDTH), BF16)],
        compiler_params=_params(2),
        name="proj_even",
    )(x, mods, ew["w_in"], ew["q_norm"], ew["kv_norm"], ew["wq"], ew["wk"], ew["wvt"], c2, s2,
      ew["sgu_g"], ew["sgu_b"], ew["w_s"], ew["b_s"], *extra_args)


def _attn_kernel(q_ref, k_ref, vt_ref, o_ref):
    lk = k_ref.shape[1]
    tq = q_ref.shape[1]
    gw = min(ATTN_QGROUP, tq)
    n_groups = tq // gw
    ck = ATTN_KEY_CHUNK
    n_chunks = lk // ck
    ones = jnp.ones((BF16_ROWS, ck), BF16)
    qs = [q_ref[0, g * gw:(g + 1) * gw, :] for g in range(n_groups)]
    scores = lambda c, g: _dot_nt(k_ref[0, c * ck:(c + 1) * ck, :], qs[g])
    pending = [[scores(c, g) for c in range(min(ATTN_LOOKAHEAD, n_chunks))] for g in range(n_groups)]
    m = [None] * n_groups
    acc = [None] * n_groups
    for c in range(n_chunks):
        v_aug = jnp.concatenate([vt_ref[0, :, c * ck:(c + 1) * ck], ones], axis=0)
        for g in range(n_groups):
            s = pending[g].pop(0)
            if c + ATTN_LOOKAHEAD < n_chunks:
                pending[g].append(scores(c + ATTN_LOOKAHEAD, g))
            m_c = jnp.max(s, axis=0, keepdims=True)
            m_new = m_c if m[g] is None else jnp.maximum(m[g], m_c)
            p = jnp.exp2(s - m_new).astype(BF16)
            pv = _dot(v_aug, p)
            acc[g] = pv if m[g] is None else acc[g] * jnp.exp2(m[g] - m_new) + pv
            m[g] = m_new
    for g in range(n_groups):
        o_ref[0, :, g * gw:(g + 1) * gw] = acc[g][:V_DIM] / acc[g][V_DIM:V_DIM + 1]


def _attention(q, k, vt, key_off, key_len, tq=2048):
    bsz, lq, _ = q.shape
    tq = _row_tile(lq, tq)
    assert key_len % ATTN_KEY_CHUNK == 0 and key_off % key_len == 0
    key_blk = key_off // key_len
    return pl.pallas_call(
        _attn_kernel,
        grid=(bsz, A_HEADS, lq // tq),
        in_specs=[pl.BlockSpec((1, tq, HEAD_PAD), lambda bi, h, i: (bi, i, h)),
                  pl.BlockSpec((1, key_len, HEAD_PAD), lambda bi, h, i: (bi, key_blk, h)),
                  pl.BlockSpec((1, V_DIM, key_len), lambda bi, h, i: (bi, h, key_blk))],
        out_specs=pl.BlockSpec((1, V_DIM, tq), lambda bi, h, i: (bi, h, i)),
        out_shape=jax.ShapeDtypeStruct((bsz, A_HEADS * V_DIM, lq), F32),
        compiler_params=_params(3),
        name="attention",
    )(q, k, vt)


def _outres_kernel(x_ref, mod_ref, at_ref, bg_ref, woa_ref, wob_ref, g_ref, b_ref, o_ref, *, alpha):
    a = at_ref[0].T.astype(BF16)
    y = _dot(a, woa_ref[...]) + _dot(bg_ref[0], wob_ref[...])
    r = alpha * x_ref[0] + mod_ref[0][5:6] * y
    o_ref[0] = _layer_norm(r, g_ref[...], b_ref[...])


def _outres(x, mods, mod_row, at, bg, woa, wob, g, b, alpha, tm=512):
    bsz, length, d = x.shape
    tm = _row_tile(length, tm)
    row = lambda bi, i: (bi, i, 0)
    return pl.pallas_call(
        functools.partial(_outres_kernel, alpha=alpha),
        grid=(bsz, length // tm),
        in_specs=[pl.BlockSpec((1, tm, d), row),
                  pl.BlockSpec((1, N_MOD, d), lambda bi, i: (mod_row(bi), 0, 0)),
                  pl.BlockSpec((1, A_HEADS * V_DIM, tm), lambda bi, i: (bi, 0, i)),
                  pl.BlockSpec((1, tm, B_WIDTH), row),
                  _const_spec(woa.shape), _const_spec(wob.shape),
                  _const_spec((1, d)), _const_spec((1, d))],
        out_specs=pl.BlockSpec((1, tm, d), row),
        out_shape=jax.ShapeDtypeStruct(x.shape, F32),
        compiler_params=_params(2),
        name="outres_even",
    )(x, mods, at, bg, woa, wob, g.reshape(1, d), b.reshape(1, d))


def _conv_kernel(x_ref, xp_ref, xn_ref, mod_ref, wpw_ref, bpw_ref, wdw_ref, bdw_ref, cg_ref, cb_ref,
                 wo_ref, bo_ref, g_ref, b_ref, o_ref, y_ref, c_ref, *, alpha):
    i = pl.program_id(1)
    last = pl.num_programs(1) - 1
    x = x_ref[0]
    tm, d = x.shape
    m = mod_ref[0]
    xcat = jnp.concatenate([xp_ref[0], x, xn_ref[0]], axis=0)
    h = _modulate(xcat, m, 1).astype(BF16)
    z = _dot(h, wpw_ref[...]) + bpw_ref[...]
    y = z[:, :d] * jax.nn.sigmoid(z[:, d:])
    r = lax.broadcasted_iota(jnp.int32, (tm + 2 * CONV_HALO, 1), 0)
    pad = ((r < CONV_HALO) & (i == 0)) | ((r >= tm + CONV_HALO) & (i == last))
    y_ref[...] = jnp.where(pad, 0.0, y)

    off = CONV_HALO - CONV_W // 2
    for rb in range(tm // CONV_ROWS):
        r0 = rb * CONV_ROWS
        for lb in range(d // LANES):
            lanes = slice(lb * LANES, (lb + 1) * LANES)
            acc = jnp.zeros((CONV_ROWS, LANES), F32) + bdw_ref[:, lanes]
            for res in range(SUBLANES):
                part = None
                for a in range((CONV_W + off) // SUBLANES + 1):
                    t = a * SUBLANES + res - off
                    if 0 <= t < CONV_W:
                        rows = slice(r0 + a * SUBLANES, r0 + a * SUBLANES + CONV_ROWS + SUBLANES)
                        term = wdw_ref[t:t + 1, lanes] * y_ref[rows, lanes]
                        part = term if part is None else part + term
                acc = acc + part[res:res + CONV_ROWS]
            c_ref[r0:r0 + CONV_ROWS, lanes] = acc

    t2 = jax.nn.silu(_layer_norm(c_ref[...], cg_ref[...], cb_ref[...])).astype(BF16)
    y2 = _dot(t2, wo_ref[...]) + bo_ref[...]
    res = alpha * x + m[5:6] * y2
    o_ref[0] = _layer_norm(res, g_ref[...], b_ref[...])


def _convmix(x, mods, mod_row, ow, g, b, alpha, tm=512):
    bsz, length, d = x.shape
    tm = _row_tile(length, tm)
    per = tm // CONV_HALO
    n_halo = length // CONV_HALO
    row = lambda bi, i: (bi, i, 0)
    return pl.pallas_call(
        functools.partial(_conv_kernel, alpha=alpha),
        grid=(bsz, length // tm),
        in_specs=[pl.BlockSpec((1, tm, d), row),
                  pl.BlockSpec((1, CONV_HALO, d), lambda bi, i: (bi, jnp.maximum(i * per - 1, 0), 0)),
                  pl.BlockSpec((1, CONV_HALO, d),
                               lambda bi, i: (bi, jnp.minimum((i + 1) * per, n_halo - 1), 0)),
                  pl.BlockSpec((1, N_MOD, d), lambda bi, i: (mod_row(bi), 0, 0)),
                  _const_spec(ow["w_pw1"].shape), _const_spec((1, 2 * d)),
                  _const_spec((CONV_W, d)), _const_spec((1, d)), _const_spec((1, d)), _const_spec((1, d)),
                  _const_spec(ow["w_out"].shape), _const_spec((1, d)),
                  _const_spec((1, d)), _const_spec((1, d))],
        out_specs=pl.BlockSpec((1, tm, d), row),
        out_shape=jax.ShapeDtypeStruct(x.shape, F32),
        scratch_shapes=[pltpu.VMEM((tm + 2 * CONV_HALO, d), F32), pltpu.VMEM((tm, d), F32)],
        compiler_params=_params(2),
        name="conv_odd",
    )(x, x, x, mods, ow["w_pw1"], ow["b_pw1"], ow["w_dw"], ow["b_dw"], ow["ln_g"], ow["ln_b"],
      ow["w_out"], ow["b_out"], g.reshape(1, d), b.reshape(1, d))


def _pad_cols(w, left, total):
    return jnp.pad(w, ((0, 0), (left, total - left - w.shape[1])))


def _even_weights(w_in, q_norm, kv_norm, w_q_up, w_kv_up, sgu_g, sgu_b, w_s, b_s, w_out):
    half = QK_ROPE // 2
    perm = jnp.concatenate([jnp.arange(0, QK_ROPE, 2), jnp.arange(1, QK_ROPE, 2)])
    swap = jnp.concatenate([perm[half:], perm[:half]])
    o_kv = Q_RANK
    o_kr = o_kv + KV_RANK
    o_b = o_kr + QK_ROPE
    kr = w_in[:, o_kr:o_b]
    w_in_ext = jnp.concatenate([w_in[:, :o_kr],
                                _pad_cols(kr[:, perm], QK_NOPE, HEAD_PAD),
                                _pad_cols(kr[:, swap], QK_NOPE, HEAD_PAD),
                                w_in[:, o_b:]], axis=1)
    wq = w_q_up.reshape(Q_RANK, A_HEADS, QK_NOPE + QK_ROPE)
    nope, rope = wq[:, :, :QK_NOPE], wq[:, :, QK_NOPE:]
    zeros_tail = jnp.zeros((Q_RANK, A_HEADS, HEAD_PAD - QK_NOPE - QK_ROPE), w_q_up.dtype)
    wq_a = jnp.concatenate([nope, rope[:, :, perm], zeros_tail], axis=2)
    wq_b = jnp.concatenate([jnp.zeros_like(nope), rope[:, :, swap], zeros_tail], axis=2)
    wq_ab = jnp.concatenate([wq_a.reshape(Q_RANK, -1), wq_b.reshape(Q_RANK, -1)], axis=1)
    wkv = w_kv_up.reshape(KV_RANK, A_HEADS, QK_NOPE + V_DIM)
    wk = jnp.concatenate([wkv[:, :, :QK_NOPE],
                          jnp.zeros((KV_RANK, A_HEADS, HEAD_PAD - QK_NOPE), w_kv_up.dtype)], axis=2)
    wvt = wkv[:, :, QK_NOPE:].reshape(KV_RANK, A_HEADS * V_DIM).T
    a_width = A_HEADS * V_DIM
    return {
        "w_in": w_in_ext.astype(BF16),
        "q_norm": q_norm.reshape(1, Q_RANK), "kv_norm": kv_norm.reshape(1, KV_RANK),
        "wq": wq_ab.astype(BF16), "wk": wk.reshape(KV_RANK, -1).astype(BF16), "wvt": wvt.astype(BF16),
        "sgu_g": sgu_g.reshape(1, B_WIDTH), "sgu_b": sgu_b.reshape(1, B_WIDTH),
        "w_s": w_s.astype(BF16),
        "b_s": jnp.broadcast_to(b_s[:, :, None], (B_GROUPS, CHUNK, B_WIDTH // B_GROUPS)),
        "wo_a": w_out[:a_width].astype(BF16), "wo_b": w_out[a_width:].astype(BF16),
    }


def _rope_tables(length):
    half = QK_ROPE // 2
    rows = jnp.repeat(jnp.arange(length // GRID_W, dtype=F32), GRID_W)
    cols = jnp.tile(jnp.arange(GRID_W, dtype=F32), length // GRID_W)
    inv = 1.0 / (ROPE_BASE ** (jnp.arange(0, half, 2, dtype=F32) / half))
    ang = jnp.concatenate([rows[:, None] * inv, cols[:, None] * inv], axis=-1)
    cos, sin = jnp.cos(ang), jnp.sin(ang)
    tail = jnp.zeros((length, HEAD_PAD - QK_NOPE - QK_ROPE), F32)
    c2 = jnp.concatenate([jnp.ones((length, QK_NOPE), F32), cos, cos, tail], axis=1)
    s2 = jnp.concatenate([jnp.zeros((length, QK_NOPE), F32), -sin, sin, tail], axis=1)
    return c2, s2


def _identity_tables(length):
    c2 = jnp.concatenate([jnp.ones((length, QK_NOPE + QK_ROPE), F32),
                          jnp.zeros((length, HEAD_PAD - QK_NOPE - QK_ROPE), F32)], axis=1)
    return c2, jnp.zeros((length, HEAD_PAD), F32)


def _ctx_needed(l, depth):
    return any(j % 2 == 0 for j in range(l, depth))


def kernel(x, c, ctx, c_ctx, w_mod, b_mod, ln_g, ln_b, ffn_w13, ffn_w2, e_w_in, e_q_norm, e_kv_norm, e_w_q_up, e_w_kv_up, e_sgu_g, e_sgu_b, e_w_s, e_b_s, e_w_out, o_w_pw1, o_b_pw1, o_w_dw, o_b_dw, o_ln_g, o_ln_b, o_w_out, o_b_out):
    bsz, length, d = x.shape
    ctx_len = ctx.shape[1]
    depth = w_mod.shape[0]
    alpha = (2 * depth) ** 0.25

    n_rows = -(-(bsz + 1) // SUBLANES) * SUBLANES
    conds = jnp.zeros((n_rows, d), F32).at[:bsz].set(c).at[bsz].set(c_ctx)
    mods = _adaln(conds, w_mod, b_mod).reshape(depth, n_rows, N_MOD, d)
    lat_row = lambda bi: bi
    ctx_row = lambda bi: bsz

    c2_lat, s2_lat = _rope_tables(length)
    c2_ctx, s2_ctx = _identity_tables(ctx_len)

    lat, cx = x, ctx
    for l in range(depth):
        ctx_in = _ctx_needed(l, depth)
        ctx_out = _ctx_needed(l + 1, depth)
        ml = mods[l]
        ffn_a = (ffn_w13[l, 0].astype(BF16), ffn_w2[l, 0].astype(BF16), ln_g[l, 0], ln_b[l, 0], alpha)
        ffn_b = (ffn_w13[l, 1].astype(BF16), ffn_w2[l, 1].astype(BF16), ln_g[l, 2], ln_b[l, 2], alpha)
        lat = _ffn(lat, ml, lat_row, 0, *ffn_a)
        if ctx_in:
            cx = _ffn(cx, ml, ctx_row, 0, *ffn_a)
        if l % 2 == 0:
            e = l // 2
            ew = _even_weights(e_w_in[e], e_q_norm[e], e_kv_norm[e], e_w_q_up[e], e_w_kv_up[e],
                               e_sgu_g[e], e_sgu_b[e], e_w_s[e], e_b_s[e], e_w_out[e])
            kv_len = length + ctx_len
            q_l, k_all, vt_all, bg_l = _proj(lat, ml, lat_row, ew, c2_lat, s2_lat, kv_len, 0)
            q_c, k_all, vt_all, bg_c = _proj(cx, ml, ctx_row, ew, c2_ctx, s2_ctx, kv_len, length,
                                             kv_bufs=(k_all, vt_all))
            at_l = _attention(q_l, k_all, vt_all, 0, kv_len)
            res = (ew["wo_a"], ew["wo_b"], ln_g[l, 1], ln_b[l, 1], alpha)
            lat = _outres(lat, ml, lat_row, at_l, bg_l, *res)
            if ctx_out:
                at_c = _attention(q_c, k_all, vt_all, length, ctx_len)
                cx = _outres(cx, ml, ctx_row, at_c, bg_c, *res)
        else:
            o = l // 2
            ow = {"w_pw1": o_w_pw1[o].astype(BF16), "b_pw1": o_b_pw1[o].reshape(1, -1),
                  "w_dw": o_w_dw[o], "b_dw": o_b_dw[o].reshape(1, d),
                  "ln_g": o_ln_g[o].reshape(1, d), "ln_b": o_ln_b[o].reshape(1, d),
                  "w_out": o_w_out[o].astype(BF16), "b_out": o_b_out[o].reshape(1, d)}
            lat = _convmix(lat, ml, lat_row, ow, ln_g[l, 1], ln_b[l, 1], alpha)
            if ctx_out:
                cx = _convmix(cx, ml, ctx_row, ow, ln_g[l, 1], ln_b[l, 1], alpha)
        lat = _ffn(lat, ml, lat_row, 2, *ffn_b)
        if ctx_out:
            cx = _ffn(cx, ml, ctx_row, 2, *ffn_b)
    return lat
```

```python
import functools
import math

import jax
import jax.numpy as jnp
from jax import lax
from jax.experimental import pallas as pl
from jax.experimental.pallas import tpu as pltpu

N_MOD = 9
D_FF = 2816
MACARON = 0.5
A_HEADS = 8
QK_NOPE = 64
QK_ROPE = 32
V_DIM = 64
Q_RANK = 384
KV_RANK = 256
ATTN_SCALE = 1.0 / math.sqrt(QK_NOPE + QK_ROPE)
ROPE_BASE = 10000.0
GRID_W = 64
CHUNK = 128
B_GROUPS = 4
B_WIDTH = 512
CONV_W = 31
LN_EPS = 1e-5
RMS_EPS = 1e-6

LANES = 128
SUBLANES = 8
BF16_ROWS = 16
ATTN_KEY_CHUNK = 256
ATTN_QGROUP = 256
ATTN_LOOKAHEAD = 2
LOG2_E = math.log2(math.e)
HEAD_PAD = 128
V7X_VMEM_LIMIT = 56 * 1024 * 1024

CONV_HALO = 16
CONV_ROWS = 128

BF16 = jnp.bfloat16
F32 = jnp.float32


def _dot(a, b):
    return jnp.dot(a, b, preferred_element_type=F32)


def _dot_nt(a, b):
    return lax.dot_general(a, b, (((1,), (1,)), ((), ())), preferred_element_type=F32)


def _layer_norm(x, g, b):
    mu = jnp.mean(x, axis=-1, keepdims=True)
    xc = x - mu
    var = jnp.mean(xc * xc, axis=-1, keepdims=True)
    return xc * lax.rsqrt(var + LN_EPS) * g + b


def _rms_norm(x, g):
    return x * lax.rsqrt(jnp.mean(x * x, axis=-1, keepdims=True) + RMS_EPS) * g


def _modulate(x, m, idx):
    return x * (1.0 + m[3 * idx + 1:3 * idx + 2]) + m[3 * idx:3 * idx + 1]


def _params(n_grid):
    return pltpu.CompilerParams(dimension_semantics=("arbitrary",) * n_grid,
                                vmem_limit_bytes=V7X_VMEM_LIMIT)


def _const_spec(shape):
    return pl.BlockSpec(shape, lambda *_: (0,) * len(shape), pipeline_mode=pl.Buffered(1))


def _row_tile(length, want):
    return min(want, length)


def _adaln_kernel(c_ref, w_ref, b_ref, o_ref):
    a = jax.nn.silu(c_ref[...]).astype(BF16)
    o_ref[0] = _dot(a, w_ref[0].astype(BF16)) + b_ref[0]


def _adaln(conds, w_mod, b_mod, tn=1152):
    depth, d, n = w_mod.shape
    rows = conds.shape[0]
    return pl.pallas_call(
        _adaln_kernel,
        grid=(depth, n // tn),
        in_specs=[pl.BlockSpec((rows, d), lambda l, j: (0, 0)),
                  pl.BlockSpec((1, d, tn), lambda l, j: (l, 0, j)),
                  pl.BlockSpec((1, 1, tn), lambda l, j: (l, 0, j))],
        out_specs=pl.BlockSpec((1, rows, tn), lambda l, j: (l, 0, j)),
        out_shape=jax.ShapeDtypeStruct((depth, rows, n), F32),
        compiler_params=_params(2),
        name="adaln",
    )(conds, w_mod, b_mod.reshape(depth, 1, n))


def _ffn_kernel(x_ref, mod_ref, w13_ref, w2_ref, g_ref, b_ref, o_ref, *, idx, alpha, chunk):
    x = x_ref[0]
    m = mod_ref[0]
    xm = _modulate(x, m, idx).astype(BF16)
    acc = jnp.zeros(x.shape, F32)
    for c in range(D_FF // chunk):
        gt = _dot(xm, w13_ref[:, c * chunk:(c + 1) * chunk])
        up = _dot(xm, w13_ref[:, D_FF + c * chunk:D_FF + (c + 1) * chunk])
        h = (gt * jax.nn.sigmoid(gt) * up).astype(BF16)
        acc = acc + _dot(h, w2_ref[c * chunk:(c + 1) * chunk, :])
    r = alpha * x + (MACARON * m[3 * idx + 2:3 * idx + 3]) * acc
    o_ref[0] = _layer_norm(r, g_ref[...], b_ref[...])


def _stacked_spec(w, lead):
    shape = (None,) * len(lead) + tuple(w.shape[len(lead):])
    index = tuple(lead) + (0,) * (w.ndim - len(lead))
    return pl.BlockSpec(shape, lambda *_: index, pipeline_mode=pl.Buffered(1))


def _ffn(x, mods, mod_row, idx, w13, w2, which, g, b, alpha, tm=1024, chunk=256):
    bsz, length, d = x.shape
    tm = _row_tile(length, tm)
    kernel = functools.partial(_ffn_kernel, idx=idx, alpha=alpha, chunk=chunk)
    return pl.pallas_call(
        kernel,
        grid=(bsz, length // tm),
        in_specs=[pl.BlockSpec((1, tm, d), lambda bi, i: (bi, i, 0)),
                  pl.BlockSpec((1, N_MOD, d), lambda bi, i: (mod_row(bi), 0, 0)),
                  _stacked_spec(w13, which), _stacked_spec(w2, which),
                  _const_spec((1, d)), _const_spec((1, d))],
        out_specs=pl.BlockSpec((1, tm, d), lambda bi, i: (bi, i, 0)),
        out_shape=jax.ShapeDtypeStruct(x.shape, F32),
        compiler_params=_params(2),
        name="ffn",
    )(x, mods, w13, w2, g.reshape(1, d), b.reshape(1, d))


def _proj_kernel(x_ref, mod_ref, win_ref, qg_ref, kvg_ref, wq_ref, wk_ref, wvt_ref, c2_ref, s2_ref,
                 sg_ref, sb_ref, ws_ref, bs_ref, q_ref, k_ref, vt_ref, bg_ref):
    x = x_ref[0]
    tm = x.shape[0]
    h = _modulate(x, mod_ref[0], 1).astype(BF16)
    z = _dot(h, win_ref[...])
    o_kv = Q_RANK
    o_kr = o_kv + KV_RANK
    o_sw = o_kr + HEAD_PAD
    o_b = o_sw + HEAD_PAD
    cq, ckv = z[:, :o_kv], z[:, o_kv:o_kr]
    zkr, zsw, zb = z[:, o_kr:o_sw], z[:, o_sw:o_b], z[:, o_b:]

    c2 = c2_ref[...]
    s2 = s2_ref[...]
    width = A_HEADS * HEAD_PAD
    cqn = _rms_norm(cq, qg_ref[...]).astype(BF16)
    qab = _dot(cqn, wq_ref[...])
    q = qab[:, :width] * jnp.tile(c2, (1, A_HEADS)) + qab[:, width:] * jnp.tile(s2, (1, A_HEADS))
    q_ref[0] = (q * (ATTN_SCALE * LOG2_E)).astype(BF16)

    ckvn = _rms_norm(ckv, kvg_ref[...]).astype(BF16)
    k_rope = zkr * c2 + zsw * s2
    k = _dot(ckvn, wk_ref[...]) + jnp.tile(k_rope, (1, A_HEADS))
    k_ref[0] = k.astype(BF16)
    vt_ref[0] = _dot_nt(wvt_ref[...], ckvn).astype(BF16)

    gz = 0.5 * zb * (1.0 + lax.erf(zb * math.sqrt(0.5)))
    u, v = gz[:, :B_WIDTH], gz[:, B_WIDTH:]
    vn = _layer_norm(v, sg_ref[...], sb_ref[...]).astype(BF16)
    gch = B_WIDTH // B_GROUPS
    for n in range(tm // CHUNK):
        rows = slice(n * CHUNK, (n + 1) * CHUNK)
        for g in range(B_GROUPS):
            cols = slice(g * gch, (g + 1) * gch)
            mixed = _dot(ws_ref[g], vn[rows, cols]) + bs_ref[g]
            bg_ref[0, rows, cols] = (u[rows, cols] * mixed).astype(BF16)


def _proj(x, mods, mod_row, ew, c2, s2, tm=512):
    bsz, length, d = x.shape
    tm = _row_tile(length, tm)
    width = A_HEADS * HEAD_PAD
    row = lambda bi, i: (bi, i, 0)
    return pl.pallas_call(
        _proj_kernel,
        grid=(bsz, length // tm),
        in_specs=[pl.BlockSpec((1, tm, d), row),
                  pl.BlockSpec((1, N_MOD, d), lambda bi, i: (mod_row(bi), 0, 0)),
                  _const_spec(ew["w_in"].shape), _const_spec((1, Q_RANK)), _const_spec((1, KV_RANK)),
                  _const_spec(ew["wq"].shape), _const_spec(ew["wk"].shape), _const_spec(ew["wvt"].shape),
                  pl.BlockSpec((tm, HEAD_PAD), lambda bi, i: (i, 0)),
                  pl.BlockSpec((tm, HEAD_PAD), lambda bi, i: (i, 0)),
                  _const_spec((1, B_WIDTH)), _const_spec((1, B_WIDTH)),
                  _const_spec(ew["w_s"].shape), _const_spec(ew["b_s"].shape)],
        out_specs=[pl.BlockSpec((1, tm, width), row),
                   pl.BlockSpec((1, tm, width), row),
                   pl.BlockSpec((1, A_HEADS * V_DIM, tm), lambda bi, i: (bi, 0, i)),
                   pl.BlockSpec((1, tm, B_WIDTH), row)],
        out_shape=[jax.ShapeDtypeStruct((bsz, length, width), BF16),
                   jax.ShapeDtypeStruct((bsz, length, width), BF16),
                   jax.ShapeDtypeStruct((bsz, A_HEADS * V_DIM, length), BF16),
                   jax.ShapeDtypeStruct((bsz, length, B_WIDTH), BF16)],
        compiler_params=_params(2),
        name="proj_even",
    )(x, mods, ew["w_in"], ew["q_norm"], ew["kv_norm"], ew["wq"], ew["wk"], ew["wvt"], c2, s2,
      ew["sgu_g"], ew["sgu_b"], ew["w_s"], ew["b_s"])


def _attn_kernel(q_ref, *refs):
    o_ref = refs[-1]
    tq = q_ref.shape[1]
    gw = min(ATTN_QGROUP, tq)
    n_groups = tq // gw
    ck = ATTN_KEY_CHUNK
    chunks = [(k_ref, vt_ref, c * ck) for k_ref, vt_ref in zip(refs[0:-1:2], refs[1:-1:2])
              for c in range(k_ref.shape[1] // ck)]
    n_chunks = len(chunks)
    ones = jnp.ones((BF16_ROWS, ck), BF16)
    qs = [q_ref[0, g * gw:(g + 1) * gw, :] for g in range(n_groups)]

    def scores(c, g):
        k_ref, _, r0 = chunks[c]
        return _dot_nt(k_ref[0, r0:r0 + ck, :], qs[g])

    pending = [[scores(c, g) for c in range(min(ATTN_LOOKAHEAD, n_chunks))] for g in range(n_groups)]
    m = [None] * n_groups
    acc = [None] * n_groups
    for c in range(n_chunks):
        _, vt_ref, r0 = chunks[c]
        v_aug = jnp.concatenate([vt_ref[0, :, r0:r0 + ck], ones], axis=0)
        for g in range(n_groups):
            s = pending[g].pop(0)
            if c + ATTN_LOOKAHEAD < n_chunks:
                pending[g].append(scores(c + ATTN_LOOKAHEAD, g))
            m_c = jnp.max(s, axis=0, keepdims=True)
            m_new = m_c if m[g] is None else jnp.maximum(m[g], m_c)
            p = jnp.exp2(s - m_new).astype(BF16)
            pv = _dot(v_aug, p)
            acc[g] = pv if m[g] is None else acc[g] * jnp.exp2(m[g] - m_new) + pv
            m[g] = m_new
    for g in range(n_groups):
        o_ref[0, :, g * gw:(g + 1) * gw] = acc[g][:V_DIM] / acc[g][V_DIM:V_DIM + 1]


def _attention(q, kv_streams, tq=1024):
    bsz, lq, _ = q.shape
    tq = _row_tile(lq, tq)
    kv_specs, kv_args = [], []
    for k, vt in kv_streams:
        lk = k.shape[1]
        assert lk % ATTN_KEY_CHUNK == 0
        kv_specs += [pl.BlockSpec((1, lk, HEAD_PAD), lambda bi, h, i: (bi, 0, h)),
                     pl.BlockSpec((1, V_DIM, lk), lambda bi, h, i: (bi, h, 0))]
        kv_args += [k, vt]
    return pl.pallas_call(
        _attn_kernel,
        grid=(bsz, A_HEADS, lq // tq),
        in_specs=[pl.BlockSpec((1, tq, HEAD_PAD), lambda bi, h, i: (bi, i, h))] + kv_specs,
        out_specs=pl.BlockSpec((1, V_DIM, tq), lambda bi, h, i: (bi, h, i)),
        out_shape=jax.ShapeDtypeStruct((bsz, A_HEADS * V_DIM, lq), F32),
        compiler_params=_params(3),
        name="attention",
    )(q, *kv_args)


def _outres_kernel(x_ref, mod_ref, at_ref, bg_ref, woa_ref, wob_ref, g_ref, b_ref, o_ref, *, alpha):
    a = at_ref[0].T.astype(BF16)
    y = _dot(a, woa_ref[...]) + _dot(bg_ref[0], wob_ref[...])
    r = alpha * x_ref[0] + mod_ref[0][5:6] * y
    o_ref[0] = _layer_norm(r, g_ref[...], b_ref[...])


def _outres(x, mods, mod_row, at, bg, woa, wob, g, b, alpha, tm=512):
    bsz, length, d = x.shape
    tm = _row_tile(length, tm)
    row = lambda bi, i: (bi, i, 0)
    return pl.pallas_call(
        functools.partial(_outres_kernel, alpha=alpha),
        grid=(bsz, length // tm),
        in_specs=[pl.BlockSpec((1, tm, d), row),
                  pl.BlockSpec((1, N_MOD, d), lambda bi, i: (mod_row(bi), 0, 0)),
                  pl.BlockSpec((1, A_HEADS * V_DIM, tm), lambda bi, i: (bi, 0, i)),
                  pl.BlockSpec((1, tm, B_WIDTH), row),
                  _const_spec(woa.shape), _const_spec(wob.shape),
                  _const_spec((1, d)), _const_spec((1, d))],
        out_specs=pl.BlockSpec((1, tm, d), row),
        out_shape=jax.ShapeDtypeStruct(x.shape, F32),
        compiler_params=_params(2),
        name="outres_even",
    )(x, mods, at, bg, woa, wob, g.reshape(1, d), b.reshape(1, d))


def _conv_kernel(x_ref, xp_ref, xn_ref, mod_ref, wpw_ref, bpw_ref, wdw_ref, bdw_ref, cg_ref, cb_ref,
                 wo_ref, bo_ref, g_ref, b_ref, o_ref, y_ref, c_ref, *, alpha):
    i = pl.program_id(1)
    last = pl.num_programs(1) - 1
    x = x_ref[0]
    tm, d = x.shape
    m = mod_ref[0]
    xcat = jnp.concatenate([xp_ref[0], x, xn_ref[0]], axis=0)
    h = _modulate(xcat, m, 1).astype(BF16)
    z = _dot(h, wpw_ref[...]) + bpw_ref[...]
    y = z[:, :d] * jax.nn.sigmoid(z[:, d:])
    r = lax.broadcasted_iota(jnp.int32, (tm + 2 * CONV_HALO, 1), 0)
    pad = ((r < CONV_HALO) & (i == 0)) | ((r >= tm + CONV_HALO) & (i == last))
    y_ref[...] = jnp.where(pad, 0.0, y)

    off = CONV_HALO - CONV_W // 2
    for rb in range(tm // CONV_ROWS):
        r0 = rb * CONV_ROWS
        for lb in range(d // LANES):
            lanes = slice(lb * LANES, (lb + 1) * LANES)
            acc = jnp.zeros((CONV_ROWS, LANES), F32) + bdw_ref[:, lanes]
            for res in range(SUBLANES):
                part = None
                for a in range((CONV_W + off) // SUBLANES + 1):
                    t = a * SUBLANES + res - off
                    if 0 <= t < CONV_W:
                        rows = slice(r0 + a * SUBLANES, r0 + a * SUBLANES + CONV_ROWS + SUBLANES)
                        term = wdw_ref[t:t + 1, lanes] * y_ref[rows, lanes]
                        part = term if part is None else part + term
                acc = acc + part[res:res + CONV_ROWS]
            c_ref[r0:r0 + CONV_ROWS, lanes] = acc

    t2 = jax.nn.silu(_layer_norm(c_ref[...], cg_ref[...], cb_ref[...])).astype(BF16)
    y2 = _dot(t2, wo_ref[...]) + bo_ref[...]
    res = alpha * x + m[5:6] * y2
    o_ref[0] = _layer_norm(res, g_ref[...], b_ref[...])


def _convmix(x, mods, mod_row, ow, g, b, alpha, tm=512):
    bsz, length, d = x.shape
    tm = _row_tile(length, tm)
    per = tm // CONV_HALO
    n_halo = length // CONV_HALO
    row = lambda bi, i: (bi, i, 0)
    return pl.pallas_call(
        functools.partial(_conv_kernel, alpha=alpha),
        grid=(bsz, length // tm),
        in_specs=[pl.BlockSpec((1, tm, d), row),
                  pl.BlockSpec((1, CONV_HALO, d), lambda bi, i: (bi, jnp.maximum(i * per - 1, 0), 0)),
                  pl.BlockSpec((1, CONV_HALO, d),
                               lambda bi, i: (bi, jnp.minimum((i + 1) * per, n_halo - 1), 0)),
                  pl.BlockSpec((1, N_MOD, d), lambda bi, i: (mod_row(bi), 0, 0)),
                  _const_spec(ow["w_pw1"].shape), _const_spec((1, 2 * d)),
                  _const_spec((CONV_W, d)), _const_spec((1, d)), _const_spec((1, d)), _const_spec((1, d)),
                  _const_spec(ow["w_out"].shape), _const_spec((1, d)),
                  _const_spec((1, d)), _const_spec((1, d))],
        out_specs=pl.BlockSpec((1, tm, d), row),
        out_shape=jax.ShapeDtypeStruct(x.shape, F32),
        scratch_shapes=[pltpu.VMEM((tm + 2 * CONV_HALO, d), F32), pltpu.VMEM((tm, d), F32)],
        compiler_params=_params(2),
        name="conv_odd",
    )(x, x, x, mods, ow["w_pw1"], ow["b_pw1"], ow["w_dw"], ow["b_dw"], ow["ln_g"], ow["ln_b"],
      ow["w_out"], ow["b_out"], g.reshape(1, d), b.reshape(1, d))


def _pad_cols(w, left, total):
    return jnp.pad(w, ((0, 0), (left, total - left - w.shape[1])))


def _even_weights(w_in, q_norm, kv_norm, w_q_up, w_kv_up, sgu_g, sgu_b, w_s, b_s, w_out):
    half = QK_ROPE // 2
    perm = jnp.concatenate([jnp.arange(0, QK_ROPE, 2), jnp.arange(1, QK_ROPE, 2)])
    swap = jnp.concatenate([perm[half:], perm[:half]])
    o_kv = Q_RANK
    o_kr = o_kv + KV_RANK
    o_b = o_kr + QK_ROPE
    kr = w_in[:, o_kr:o_b]
    w_in_ext = jnp.concatenate([w_in[:, :o_kr],
                                _pad_cols(kr[:, perm], QK_NOPE, HEAD_PAD),
                                _pad_cols(kr[:, swap], QK_NOPE, HEAD_PAD),
                                w_in[:, o_b:]], axis=1)
    wq = w_q_up.reshape(Q_RANK, A_HEADS, QK_NOPE + QK_ROPE)
    nope, rope = wq[:, :, :QK_NOPE], wq[:, :, QK_NOPE:]
    zeros_tail = jnp.zeros((Q_RANK, A_HEADS, HEAD_PAD - QK_NOPE - QK_ROPE), w_q_up.dtype)
    wq_a = jnp.concatenate([nope, rope[:, :, perm], zeros_tail], axis=2)
    wq_b = jnp.concatenate([jnp.zeros_like(nope), rope[:, :, swap], zeros_tail], axis=2)
    wq_ab = jnp.concatenate([wq_a.reshape(Q_RANK, -1), wq_b.reshape(Q_RANK, -1)], axis=1)
    wkv = w_kv_up.reshape(KV_RANK, A_HEADS, QK_NOPE + V_DIM)
    wk = jnp.concatenate([wkv[:, :, :QK_NOPE],
                          jnp.zeros((KV_RANK, A_HEADS, HEAD_PAD - QK_NOPE), w_kv_up.dtype)], axis=2)
    wvt = wkv[:, :, QK_NOPE:].reshape(KV_RANK, A_HEADS * V_DIM).T
    a_width = A_HEADS * V_DIM
    return {
        "w_in": w_in_ext.astype(BF16),
        "q_norm": q_norm.reshape(1, Q_RANK), "kv_norm": kv_norm.reshape(1, KV_RANK),
        "wq": wq_ab.astype(BF16), "wk": wk.reshape(KV_RANK, -1).astype(BF16), "wvt": wvt.astype(BF16),
        "sgu_g": sgu_g.reshape(1, B_WIDTH), "sgu_b": sgu_b.reshape(1, B_WIDTH),
        "w_s": w_s.astype(BF16),
        "b_s": jnp.broadcast_to(b_s[:, :, None], (B_GROUPS, CHUNK, B_WIDTH // B_GROUPS)),
        "wo_a": w_out[:a_width].astype(BF16), "wo_b": w_out[a_width:].astype(BF16),
    }


def _rope_tables(length):
    half = QK_ROPE // 2
    rows = jnp.repeat(jnp.arange(length // GRID_W, dtype=F32), GRID_W)
    cols = jnp.tile(jnp.arange(GRID_W, dtype=F32), length // GRID_W)
    inv = 1.0 / (ROPE_BASE ** (jnp.arange(0, half, 2, dtype=F32) / half))
    ang = jnp.concatenate([rows[:, None] * inv, cols[:, None] * inv], axis=-1)
    cos, sin = jnp.cos(ang), jnp.sin(ang)
    tail = jnp.zeros((length, HEAD_PAD - QK_NOPE - QK_ROPE), F32)
    c2 = jnp.concatenate([jnp.ones((length, QK_NOPE), F32), cos, cos, tail], axis=1)
    s2 = jnp.concatenate([jnp.zeros((length, QK_NOPE), F32), -sin, sin, tail], axis=1)
    return c2, s2


def _identity_tables(length):
    c2 = jnp.concatenate([jnp.ones((length, QK_NOPE + QK_ROPE), F32),
                          jnp.zeros((length, HEAD_PAD - QK_NOPE - QK_ROPE), F32)], axis=1)
    return c2, jnp.zeros((length, HEAD_PAD), F32)


def _ctx_needed(l, depth):
    return any(j % 2 == 0 for j in range(l, depth))


def kernel(x, c, ctx, c_ctx, w_mod, b_mod, ln_g, ln_b, ffn_w13, ffn_w2, e_w_in, e_q_norm, e_kv_norm, e_w_q_up, e_w_kv_up, e_sgu_g, e_sgu_b, e_w_s, e_b_s, e_w_out, o_w_pw1, o_b_pw1, o_w_dw, o_b_dw, o_ln_g, o_ln_b, o_w_out, o_b_out):
    bsz, length, d = x.shape
    ctx_len = ctx.shape[1]
    depth = w_mod.shape[0]
    alpha = (2 * depth) ** 0.25

    n_rows = -(-(bsz + 1) // SUBLANES) * SUBLANES
    conds = jnp.zeros((n_rows, d), F32).at[:bsz].set(c).at[bsz].set(c_ctx)
    mods = _adaln(conds, w_mod, b_mod).reshape(depth, n_rows, N_MOD, d)
    lat_row = lambda bi: bi
    ctx_row = lambda bi: bsz

    w13 = ffn_w13.astype(BF16)
    w2 = ffn_w2.astype(BF16)
    c2_lat, s2_lat = _rope_tables(length)
    c2_ctx, s2_ctx = _identity_tables(ctx_len)

    lat, cx = x, ctx
    for l in range(depth):
        ctx_in = _ctx_needed(l, depth)
        ctx_out = _ctx_needed(l + 1, depth)
        ml = mods[l]
        ffn_a = (w13, w2, (l, 0), ln_g[l, 0], ln_b[l, 0], alpha)
        ffn_b = (w13, w2, (l, 1), ln_g[l, 2], ln_b[l, 2], alpha)
        lat = _ffn(lat, ml, lat_row, 0, *ffn_a)
        if ctx_in:
            cx = _ffn(cx, ml, ctx_row, 0, *ffn_a)
        if l % 2 == 0:
            e = l // 2
            ew = _even_weights(e_w_in[e], e_q_norm[e], e_kv_norm[e], e_w_q_up[e], e_w_kv_up[e],
                               e_sgu_g[e], e_sgu_b[e], e_w_s[e], e_b_s[e], e_w_out[e])
            q_l, k_l, vt_l, bg_l = _proj(lat, ml, lat_row, ew, c2_lat, s2_lat)
            q_c, k_c, vt_c, bg_c = _proj(cx, ml, ctx_row, ew, c2_ctx, s2_ctx)
            at_l = _attention(q_l, [(k_l, vt_l), (k_c, vt_c)])
            res = (ew["wo_a"], ew["wo_b"], ln_g[l, 1], ln_b[l, 1], alpha)
            lat = _outres(lat, ml, lat_row, at_l, bg_l, *res)
            if ctx_out:
                at_c = _attention(q_c, [(k_c, vt_c)])
                cx = _outres(cx, ml, ctx_row, at_c, bg_c, *res)
        else:
            o = l // 2
            ow = {"w_pw1": o_w_pw1[o].astype(BF16), "b_pw1": o_b_pw1[o].reshape(1, -1),
                  "w_dw": o_w_dw[o], "b_dw": o_b_dw[o].reshape(1, d),
                  "ln_g": o_ln_g[o].reshape(1, d), "ln_b": o_ln_b[o].reshape(1, d),
                  "w_out": o_w_out[o].astype(BF16), "b_out": o_b_out[o].reshape(1, d)}
            lat = _convmix(lat, ml, lat_row, ow, ln_g[l, 1], ln_b[l, 1], alpha)
            if ctx_out:
                cx = _convmix(cx, ml, ctx_row, ow, ln_g[l, 1], ln_b[l, 1], alpha)
        lat = _ffn(lat, ml, lat_row, 2, *ffn_b)
        if ctx_out:
            cx = _ffn(cx, ml, ctx_row, 2, *ffn_b)
    return lat
```

```python
import functools
import math

import jax
import jax.numpy as jnp
from jax import lax
from jax.experimental import pallas as pl
from jax.experimental.pallas import tpu as pltpu

N_MOD = 9
D_FF = 2816
MACARON = 0.5
A_HEADS = 8
QK_NOPE = 64
QK_ROPE = 32
V_DIM = 64
Q_RANK = 384
KV_RANK = 256
ATTN_SCALE = 1.0 / math.sqrt(QK_NOPE + QK_ROPE)
ROPE_BASE = 10000.0
GRID_W = 64
CHUNK = 128
B_GROUPS = 4
B_WIDTH = 512
CONV_W = 31
LN_EPS = 1e-5
RMS_EPS = 1e-6

LANES = 128
SUBLANES = 8
BF16_ROWS = 16
ATTN_KEY_CHUNK = 256
ATTN_QGROUP = 256
ATTN_LOOKAHEAD = 2
LOG2_E = math.log2(math.e)
HEAD_PAD = 128
V7X_VMEM_LIMIT = 56 * 1024 * 1024

CONV_HALO = 16
CONV_ROWS = 128

BF16 = jnp.bfloat16
F32 = jnp.float32


def _dot(a, b):
    return jnp.dot(a, b, preferred_element_type=F32)


def _dot_nt(a, b):
    return lax.dot_general(a, b, (((1,), (1,)), ((), ())), preferred_element_type=F32)


def _layer_norm(x, g, b):
    mu = jnp.mean(x, axis=-1, keepdims=True)
    xc = x - mu
    var = jnp.mean(xc * xc, axis=-1, keepdims=True)
    return xc * lax.rsqrt(var + LN_EPS) * g + b


def _rms_norm(x, g):
    return x * lax.rsqrt(jnp.mean(x * x, axis=-1, keepdims=True) + RMS_EPS) * g


def _modulate(x, m, idx):
    return x * (1.0 + m[3 * idx + 1:3 * idx + 2]) + m[3 * idx:3 * idx + 1]


def _params(n_grid):
    return pltpu.CompilerParams(dimension_semantics=("arbitrary",) * n_grid,
                                vmem_limit_bytes=V7X_VMEM_LIMIT)


def _const_spec(shape):
    return pl.BlockSpec(shape, lambda *_: (0,) * len(shape), pipeline_mode=pl.Buffered(1))


def _row_tile(length, want):
    return min(want, length)


def _adaln_kernel(c_ref, w_ref, b_ref, o_ref):
    a = jax.nn.silu(c_ref[...]).astype(BF16)
    o_ref[0] = _dot(a, w_ref[0].astype(BF16)) + b_ref[0]


def _adaln(conds, w_mod, b_mod, tn=1152):
    depth, d, n = w_mod.shape
    rows = conds.shape[0]
    return pl.pallas_call(
        _adaln_kernel,
        grid=(depth, n // tn),
        in_specs=[pl.BlockSpec((rows, d), lambda l, j: (0, 0)),
                  pl.BlockSpec((1, d, tn), lambda l, j: (l, 0, j)),
                  pl.BlockSpec((1, 1, tn), lambda l, j: (l, 0, j))],
        out_specs=pl.BlockSpec((1, rows, tn), lambda l, j: (l, 0, j)),
        out_shape=jax.ShapeDtypeStruct((depth, rows, n), F32),
        compiler_params=_params(2),
        name="adaln",
    )(conds, w_mod, b_mod.reshape(depth, 1, n))


def _ffn_kernel(x_ref, mod_ref, w13_ref, w2_ref, g_ref, b_ref, o_ref, *, idx, alpha, chunk):
    x = x_ref[0]
    m = mod_ref[0]
    xm = _modulate(x, m, idx).astype(BF16)
    acc = jnp.zeros(x.shape, F32)
    for c in range(D_FF // chunk):
        gt = _dot(xm, w13_ref[:, c * chunk:(c + 1) * chunk])
        up = _dot(xm, w13_ref[:, D_FF + c * chunk:D_FF + (c + 1) * chunk])
        h = (gt * jax.nn.sigmoid(gt) * up).astype(BF16)
        acc = acc + _dot(h, w2_ref[c * chunk:(c + 1) * chunk, :])
    r = alpha * x + (MACARON * m[3 * idx + 2:3 * idx + 3]) * acc
    o_ref[0] = _layer_norm(r, g_ref[...], b_ref[...])


def _stacked_spec(w, lead):
    shape = (None,) * len(lead) + tuple(w.shape[len(lead):])
    index = tuple(lead) + (0,) * (w.ndim - len(lead))
    return pl.BlockSpec(shape, lambda *_: index, pipeline_mode=pl.Buffered(1))


def _ffn(x, mods, mod_row, idx, w13, w2, which, g, b, alpha, tm=1024, chunk=256):
    bsz, length, d = x.shape
    tm = _row_tile(length, tm)
    kernel = functools.partial(_ffn_kernel, idx=idx, alpha=alpha, chunk=chunk)
    return pl.pallas_call(
        kernel,
        grid=(bsz, length // tm),
        in_specs=[pl.BlockSpec((1, tm, d), lambda bi, i: (bi, i, 0)),
                  pl.BlockSpec((1, N_MOD, d), lambda bi, i: (mod_row(bi), 0, 0)),
                  _stacked_spec(w13, which), _stacked_spec(w2, which),
                  _const_spec((1, d)), _const_spec((1, d))],
        out_specs=pl.BlockSpec((1, tm, d), lambda bi, i: (bi, i, 0)),
        out_shape=jax.ShapeDtypeStruct(x.shape, F32),
        compiler_params=_params(2),
        name="ffn",
    )(x, mods, w13, w2, g.reshape(1, d), b.reshape(1, d))


def _proj_kernel(x_ref, mod_ref, win_ref, qg_ref, kvg_ref, wq_ref, wk_ref, wvt_ref, c2_ref, s2_ref,
                 sg_ref, sb_ref, ws_ref, bs_ref, q_ref, k_ref, vt_ref, bg_ref):
    x = x_ref[0]
    tm = x.shape[0]
    h = _modulate(x, mod_ref[0], 1).astype(BF16)
    z = _dot(h, win_ref[...])
    o_kv = Q_RANK
    o_kr = o_kv + KV_RANK
    o_sw = o_kr + HEAD_PAD
    o_b = o_sw + HEAD_PAD
    cq, ckv = z[:, :o_kv], z[:, o_kv:o_kr]
    zkr, zsw, zb = z[:, o_kr:o_sw], z[:, o_sw:o_b], z[:, o_b:]

    c2 = c2_ref[...]
    s2 = s2_ref[...]
    width = A_HEADS * HEAD_PAD
    cqn = _rms_norm(cq, qg_ref[...]).astype(BF16)
    qab = _dot(cqn, wq_ref[...])
    q = qab[:, :width] * jnp.tile(c2, (1, A_HEADS)) + qab[:, width:] * jnp.tile(s2, (1, A_HEADS))
    q_ref[0] = (q * (ATTN_SCALE * LOG2_E)).astype(BF16)

    ckvn = _rms_norm(ckv, kvg_ref[...]).astype(BF16)
    k_rope = zkr * c2 + zsw * s2
    k = _dot(ckvn, wk_ref[...]) + jnp.tile(k_rope, (1, A_HEADS))
    k_ref[0] = k.astype(BF16)
    vt_ref[0] = _dot_nt(wvt_ref[...], ckvn).astype(BF16)

    gz = 0.5 * zb * (1.0 + lax.erf(zb * math.sqrt(0.5)))
    u, v = gz[:, :B_WIDTH], gz[:, B_WIDTH:]
    vn = _layer_norm(v, sg_ref[...], sb_ref[...]).astype(BF16)
    gch = B_WIDTH // B_GROUPS
    for n in range(tm // CHUNK):
        rows = slice(n * CHUNK, (n + 1) * CHUNK)
        for g in range(B_GROUPS):
            cols = slice(g * gch, (g + 1) * gch)
            mixed = _dot(ws_ref[g], vn[rows, cols]) + bs_ref[g]
            bg_ref[0, rows, cols] = (u[rows, cols] * mixed).astype(BF16)


def _proj(x, mods, mod_row, ew, c2, s2, tm=512):
    bsz, length, d = x.shape
    tm = _row_tile(length, tm)
    width = A_HEADS * HEAD_PAD
    row = lambda bi, i: (bi, i, 0)
    return pl.pallas_call(
        _proj_kernel,
        grid=(bsz, length // tm),
        in_specs=[pl.BlockSpec((1, tm, d), row),
                  pl.BlockSpec((1, N_MOD, d), lambda bi, i: (mod_row(bi), 0, 0)),
                  _const_spec(ew["w_in"].shape), _const_spec((1, Q_RANK)), _const_spec((1, KV_RANK)),
                  _const_spec(ew["wq"].shape), _const_spec(ew["wk"].shape), _const_spec(ew["wvt"].shape),
                  pl.BlockSpec((tm, HEAD_PAD), lambda bi, i: (i, 0)),
                  pl.BlockSpec((tm, HEAD_PAD), lambda bi, i: (i, 0)),
                  _const_spec((1, B_WIDTH)), _const_spec((1, B_WIDTH)),
                  _const_spec(ew["w_s"].shape), _const_spec(ew["b_s"].shape)],
        out_specs=[pl.BlockSpec((1, tm, width), row),
                   pl.BlockSpec((1, tm, width), row),
                   pl.BlockSpec((1, A_HEADS * V_DIM, tm), lambda bi, i: (bi, 0, i)),
                   pl.BlockSpec((1, tm, B_WIDTH), row)],
        out_shape=[jax.ShapeDtypeStruct((bsz, length, width), BF16),
                   jax.ShapeDtypeStruct((bsz, length, width), BF16),
                   jax.ShapeDtypeStruct((bsz, A_HEADS * V_DIM, length), BF16),
                   jax.ShapeDtypeStruct((bsz, length, B_WIDTH), BF16)],
        compiler_params=_params(2),
        name="proj_even",
    )(x, mods, ew["w_in"], ew["q_norm"], ew["kv_norm"], ew["wq"], ew["wk"], ew["wvt"], c2, s2,
      ew["sgu_g"], ew["sgu_b"], ew["w_s"], ew["b_s"])


def _attn_kernel(q_ref, *refs):
    o_ref = refs[-1]
    tq = q_ref.shape[1]
    gw = min(ATTN_QGROUP, tq)
    n_groups = tq // gw
    ck = ATTN_KEY_CHUNK
    chunks = [(k_ref, vt_ref, c * ck) for k_ref, vt_ref in zip(refs[0:-1:2], refs[1:-1:2])
              for c in range(k_ref.shape[1] // ck)]
    n_chunks = len(chunks)
    ones = jnp.ones((BF16_ROWS, ck), BF16)
    qs = [q_ref[0, g * gw:(g + 1) * gw, :] for g in range(n_groups)]

    def scores(c, g):
        k_ref, _, r0 = chunks[c]
        return _dot_nt(k_ref[0, r0:r0 + ck, :], qs[g])

    pending = [[scores(c, g) for c in range(min(ATTN_LOOKAHEAD, n_chunks))] for g in range(n_groups)]
    m = [None] * n_groups
    acc = [None] * n_groups
    for c in range(n_chunks):
        _, vt_ref, r0 = chunks[c]
        v_aug = jnp.concatenate([vt_ref[0, :, r0:r0 + ck], ones], axis=0)
        for g in range(n_groups):
            s = pending[g].pop(0)
            if c + ATTN_LOOKAHEAD < n_chunks:
                pending[g].append(scores(c + ATTN_LOOKAHEAD, g))
            m_c = jnp.max(s, axis=0, keepdims=True)
            m_new = m_c if m[g] is None else jnp.maximum(m[g], m_c)
            p = jnp.exp2(s - m_new).astype(BF16)
            pv = _dot(v_aug, p)
            acc[g] = pv if m[g] is None else acc[g] * jnp.exp2(m[g] - m_new) + pv
            m[g] = m_new
    for g in range(n_groups):
        o_ref[0, :, g * gw:(g + 1) * gw] = acc[g][:V_DIM] / acc[g][V_DIM:V_DIM + 1]


def _attention(q, kv_streams, tq=1024):
    bsz, lq, _ = q.shape
    tq = _row_tile(lq, tq)
    kv_specs, kv_args = [], []
    for k, vt in kv_streams:
        lk = k.shape[1]
        assert lk % ATTN_KEY_CHUNK == 0
        kv_specs += [pl.BlockSpec((1, lk, HEAD_PAD), lambda bi, h, i: (bi, 0, h)),
                     pl.BlockSpec((1, V_DIM, lk), lambda bi, h, i: (bi, h, 0))]
        kv_args += [k, vt]
    return pl.pallas_call(
        _attn_kernel,
        grid=(bsz, A_HEADS, lq // tq),
        in_specs=[pl.BlockSpec((1, tq, HEAD_PAD), lambda bi, h, i: (bi, i, h))] + kv_specs,
        out_specs=pl.BlockSpec((1, V_DIM, tq), lambda bi, h, i: (bi, h, i)),
        out_shape=jax.ShapeDtypeStruct((bsz, A_HEADS * V_DIM, lq), F32),
        compiler_params=_params(3),
        name="attention",
    )(q, *kv_args)


def _outres_kernel(x_ref, mod_ref, at_ref, bg_ref, woa_ref, wob_ref, g_ref, b_ref, o_ref, *, alpha):
    a = at_ref[0].T.astype(BF16)
    y = _dot(a, woa_ref[...]) + _dot(bg_ref[0], wob_ref[...])
    r = alpha * x_ref[0] + mod_ref[0][5:6] * y
    o_ref[0] = _layer_norm(r, g_ref[...], b_ref[...])


def _outres(x, mods, mod_row, at, bg, woa, wob, g, b, alpha, tm=512):
    bsz, length, d = x.shape
    tm = _row_tile(length, tm)
    row = lambda bi, i: (bi, i, 0)
    return pl.pallas_call(
        functools.partial(_outres_kernel, alpha=alpha),
        grid=(bsz, length // tm),
        in_specs=[pl.BlockSpec((1, tm, d), row),
                  pl.BlockSpec((1, N_MOD, d), lambda bi, i: (mod_row(bi), 0, 0)),
                  pl.BlockSpec((1, A_HEADS * V_DIM, tm), lambda bi, i: (bi, 0, i)),
                  pl.BlockSpec((1, tm, B_WIDTH), row),
                  _const_spec(woa.shape), _const_spec(wob.shape),
                  _const_spec((1, d)), _const_spec((1, d))],
        out_specs=pl.BlockSpec((1, tm, d), row),
        out_shape=jax.ShapeDtypeStruct(x.shape, F32),
        compiler_params=_params(2),
        name="outres_even",
    )(x, mods, at, bg, woa, wob, g.reshape(1, d), b.reshape(1, d))


def _conv_kernel(x_ref, xp_ref, xn_ref, mod_ref, wpw_ref, bpw_ref, wdw_ref, bdw_ref, cg_ref, cb_ref,
                 wo_ref, bo_ref, g_ref, b_ref, o_ref, y_ref, c_ref, *, alpha):
    i = pl.program_id(1)
    last = pl.num_programs(1) - 1
    x = x_ref[0]
    tm, d = x.shape
    m = mod_ref[0]
    xcat = jnp.concatenate([xp_ref[0], x, xn_ref[0]], axis=0)
    h = _modulate(xcat, m, 1).astype(BF16)
    z = _dot(h, wpw_ref[...]) + bpw_ref[...]
    y = z[:, :d] * jax.nn.sigmoid(z[:, d:])
    r = lax.broadcasted_iota(jnp.int32, (tm + 2 * CONV_HALO, 1), 0)
    pad = ((r < CONV_HALO) & (i == 0)) | ((r >= tm + CONV_HALO) & (i == last))
    y_ref[...] = jnp.where(pad, 0.0, y)

    off = CONV_HALO - CONV_W // 2
    for rb in range(tm // CONV_ROWS):
        r0 = rb * CONV_ROWS
        for lb in range(d // LANES):
            lanes = slice(lb * LANES, (lb + 1) * LANES)
            acc = jnp.zeros((CONV_ROWS, LANES), F32) + bdw_ref[:, lanes]
            for res in range(SUBLANES):
                part = None
                for a in range((CONV_W + off) // SUBLANES + 1):
                    t = a * SUBLANES + res - off
                    if 0 <= t < CONV_W:
                        rows = slice(r0 + a * SUBLANES, r0 + a * SUBLANES + CONV_ROWS + SUBLANES)
                        term = wdw_ref[t:t + 1, lanes] * y_ref[rows, lanes]
                        part = term if part is None else part + term
                acc = acc + part[res:res + CONV_ROWS]
            c_ref[r0:r0 + CONV_ROWS, lanes] = acc

    t2 = jax.nn.silu(_layer_norm(c_ref[...], cg_ref[...], cb_ref[...])).astype(BF16)
    y2 = _dot(t2, wo_ref[...]) + bo_ref[...]
    res = alpha * x + m[5:6] * y2
    o_ref[0] = _layer_norm(res, g_ref[...], b_ref[...])


def _convmix(x, mods, mod_row, ow, g, b, alpha, tm=1024):
    bsz, length, d = x.shape
    tm = _row_tile(length, tm)
    per = tm // CONV_HALO
    n_halo = length // CONV_HALO
    row = lambda bi, i: (bi, i, 0)
    return pl.pallas_call(
        functools.partial(_conv_kernel, alpha=alpha),
        grid=(bsz, length // tm),
        in_specs=[pl.BlockSpec((1, tm, d), row),
                  pl.BlockSpec((1, CONV_HALO, d), lambda bi, i: (bi, jnp.maximum(i * per - 1, 0), 0)),
                  pl.BlockSpec((1, CONV_HALO, d),
                               lambda bi, i: (bi, jnp.minimum((i + 1) * per, n_halo - 1), 0)),
                  pl.BlockSpec((1, N_MOD, d), lambda bi, i: (mod_row(bi), 0, 0)),
                  _const_spec(ow["w_pw1"].shape), _const_spec((1, 2 * d)),
                  _const_spec((CONV_W, d)), _const_spec((1, d)), _const_spec((1, d)), _const_spec((1, d)),
                  _const_spec(ow["w_out"].shape), _const_spec((1, d)),
                  _const_spec((1, d)), _const_spec((1, d))],
        out_specs=pl.BlockSpec((1, tm, d), row),
        out_shape=jax.ShapeDtypeStruct(x.shape, F32),
        scratch_shapes=[pltpu.VMEM((tm + 2 * CONV_HALO, d), F32), pltpu.VMEM((tm, d), F32)],
        compiler_params=_params(2),
        name="conv_odd",
    )(x, x, x, mods, ow["w_pw1"], ow["b_pw1"], ow["w_dw"], ow["b_dw"], ow["ln_g"], ow["ln_b"],
      ow["w_out"], ow["b_out"], g.reshape(1, d), b.reshape(1, d))


def _pad_cols(w, left, total):
    return jnp.pad(w, ((0, 0), (left, total - left - w.shape[1])))


def _even_weights(w_in, q_norm, kv_norm, w_q_up, w_kv_up, sgu_g, sgu_b, w_s, b_s, w_out):
    half = QK_ROPE // 2
    perm = jnp.concatenate([jnp.arange(0, QK_ROPE, 2), jnp.arange(1, QK_ROPE, 2)])
    swap = jnp.concatenate([perm[half:], perm[:half]])
    o_kv = Q_RANK
    o_kr = o_kv + KV_RANK
    o_b = o_kr + QK_ROPE
    kr = w_in[:, o_kr:o_b]
    w_in_ext = jnp.concatenate([w_in[:, :o_kr],
                                _pad_cols(kr[:, perm], QK_NOPE, HEAD_PAD),
                                _pad_cols(kr[:, swap], QK_NOPE, HEAD_PAD),
                                w_in[:, o_b:]], axis=1)
    wq = w_q_up.reshape(Q_RANK, A_HEADS, QK_NOPE + QK_ROPE)
    nope, rope = wq[:, :, :QK_NOPE], wq[:, :, QK_NOPE:]
    zeros_tail = jnp.zeros((Q_RANK, A_HEADS, HEAD_PAD - QK_NOPE - QK_ROPE), w_q_up.dtype)
    wq_a = jnp.concatenate([nope, rope[:, :, perm], zeros_tail], axis=2)
    wq_b = jnp.concatenate([jnp.zeros_like(nope), rope[:, :, swap], zeros_tail], axis=2)
    wq_ab = jnp.concatenate([wq_a.reshape(Q_RANK, -1), wq_b.reshape(Q_RANK, -1)], axis=1)
    wkv = w_kv_up.reshape(KV_RANK, A_HEADS, QK_NOPE + V_DIM)
    wk = jnp.concatenate([wkv[:, :, :QK_NOPE],
                          jnp.zeros((KV_RANK, A_HEADS, HEAD_PAD - QK_NOPE), w_kv_up.dtype)], axis=2)
    wvt = wkv[:, :, QK_NOPE:].reshape(KV_RANK, A_HEADS * V_DIM).T
    a_width = A_HEADS * V_DIM
    return {
        "w_in": w_in_ext.astype(BF16),
        "q_norm": q_norm.reshape(1, Q_RANK), "kv_norm": kv_norm.reshape(1, KV_RANK),
        "wq": wq_ab.astype(BF16), "wk": wk.reshape(KV_RANK, -1).astype(BF16), "wvt": wvt.astype(BF16),
        "sgu_g": sgu_g.reshape(1, B_WIDTH), "sgu_b": sgu_b.reshape(1, B_WIDTH),
        "w_s": w_s.astype(BF16),
        "b_s": jnp.broadcast_to(b_s[:, :, None], (B_GROUPS, CHUNK, B_WIDTH // B_GROUPS)),
        "wo_a": w_out[:a_width].astype(BF16), "wo_b": w_out[a_width:].astype(BF16),
    }


def _rope_tables(length):
    half = QK_ROPE // 2
    rows = jnp.repeat(jnp.arange(length // GRID_W, dtype=F32), GRID_W)
    cols = jnp.tile(jnp.arange(GRID_W, dtype=F32), length // GRID_W)
    inv = 1.0 / (ROPE_BASE ** (jnp.arange(0, half, 2, dtype=F32) / half))
    ang = jnp.concatenate([rows[:, None] * inv, cols[:, None] * inv], axis=-1)
    cos, sin = jnp.cos(ang), jnp.sin(ang)
    tail = jnp.zeros((length, HEAD_PAD - QK_NOPE - QK_ROPE), F32)
    c2 = jnp.concatenate([jnp.ones((length, QK_NOPE), F32), cos, cos, tail], axis=1)
    s2 = jnp.concatenate([jnp.zeros((length, QK_NOPE), F32), -sin, sin, tail], axis=1)
    return c2, s2


def _identity_tables(length):
    c2 = jnp.concatenate([jnp.ones((length, QK_NOPE + QK_ROPE), F32),
                          jnp.zeros((length, HEAD_PAD - QK_NOPE - QK_ROPE), F32)], axis=1)
    return c2, jnp.zeros((length, HEAD_PAD), F32)


def _ctx_needed(l, depth):
    return any(j % 2 == 0 for j in range(l, depth))


def kernel(x, c, ctx, c_ctx, w_mod, b_mod, ln_g, ln_b, ffn_w13, ffn_w2, e_w_in, e_q_norm, e_kv_norm, e_w_q_up, e_w_kv_up, e_sgu_g, e_sgu_b, e_w_s, e_b_s, e_w_out, o_w_pw1, o_b_pw1, o_w_dw, o_b_dw, o_ln_g, o_ln_b, o_w_out, o_b_out):
    bsz, length, d = x.shape
    ctx_len = ctx.shape[1]
    depth = w_mod.shape[0]
    alpha = (2 * depth) ** 0.25

    n_rows = -(-(bsz + 1) // SUBLANES) * SUBLANES
    conds = jnp.zeros((n_rows, d), F32).at[:bsz].set(c).at[bsz].set(c_ctx)
    mods = _adaln(conds, w_mod, b_mod).reshape(depth, n_rows, N_MOD, d)
    lat_row = lambda bi: bi
    ctx_row = lambda bi: bsz

    w13 = ffn_w13.astype(BF16)
    w2 = ffn_w2.astype(BF16)
    c2_lat, s2_lat = _rope_tables(length)
    c2_ctx, s2_ctx = _identity_tables(ctx_len)

    lat, cx = x, ctx
    for l in range(depth):
        ctx_in = _ctx_needed(l, depth)
        ctx_out = _ctx_needed(l + 1, depth)
        ml = mods[l]
        ffn_a = (w13, w2, (l, 0), ln_g[l, 0], ln_b[l, 0], alpha)
        ffn_b = (w13, w2, (l, 1), ln_g[l, 2], ln_b[l, 2], alpha)
        lat = _ffn(lat, ml, lat_row, 0, *ffn_a)
        if ctx_in:
            cx = _ffn(cx, ml, ctx_row, 0, *ffn_a)
        if l % 2 == 0:
            e = l // 2
            ew = _even_weights(e_w_in[e], e_q_norm[e], e_kv_norm[e], e_w_q_up[e], e_w_kv_up[e],
                               e_sgu_g[e], e_sgu_b[e], e_w_s[e], e_b_s[e], e_w_out[e])
            q_l, k_l, vt_l, bg_l = _proj(lat, ml, lat_row, ew, c2_lat, s2_lat)
            q_c, k_c, vt_c, bg_c = _proj(cx, ml, ctx_row, ew, c2_ctx, s2_ctx)
            at_l = _attention(q_l, [(k_l, vt_l), (k_c, vt_c)])
            res = (ew["wo_a"], ew["wo_b"], ln_g[l, 1], ln_b[l, 1], alpha)
            lat = _outres(lat, ml, lat_row, at_l, bg_l, *res)
            if ctx_out:
                at_c = _attention(q_c, [(k_c, vt_c)])
                cx = _outres(cx, ml, ctx_row, at_c, bg_c, *res)
        else:
            o = l // 2
            ow = {"w_pw1": o_w_pw1[o].astype(BF16), "b_pw1": o_b_pw1[o].reshape(1, -1),
                  "w_dw": o_w_dw[o], "b_dw": o_b_dw[o].reshape(1, d),
                  "ln_g": o_ln_g[o].reshape(1, d), "ln_b": o_ln_b[o].reshape(1, d),
                  "w_out": o_w_out[o].astype(BF16), "b_out": o_b_out[o].reshape(1, d)}
            lat = _convmix(lat, ml, lat_row, ow, ln_g[l, 1], ln_b[l, 1], alpha)
            if ctx_out:
                cx = _convmix(cx, ml, ctx_row, ow, ln_g[l, 1], ln_b[l, 1], alpha)
        lat = _ffn(lat, ml, lat_row, 2, *ffn_b)
        if ctx_out:
            cx = _ffn(cx, ml, ctx_row, 2, *ffn_b)
    return lat
```

```python
import functools
import math

import jax
import jax.numpy as jnp
from jax import lax
from jax.experimental import pallas as pl
from jax.experimental.pallas import tpu as pltpu

N_MOD = 9
D_FF = 2816
MACARON = 0.5
A_HEADS = 8
QK_NOPE = 64
QK_ROPE = 32
V_DIM = 64
Q_RANK = 384
KV_RANK = 256
ATTN_SCALE = 1.0 / math.sqrt(QK_NOPE + QK_ROPE)
ROPE_BASE = 10000.0
GRID_W = 64
CHUNK = 128
B_GROUPS = 4
B_WIDTH = 512
CONV_W = 31
LN_EPS = 1e-5
RMS_EPS = 1e-6

LANES = 128
SUBLANES = 8
BF16_ROWS = 16
ATTN_KEY_CHUNK = 256
ATTN_QGROUP = 256
ATTN_LOOKAHEAD = 2
LOG2_E = math.log2(math.e)
HEAD_PAD = 128
V7X_VMEM_LIMIT = 56 * 1024 * 1024

CONV_HALO = 16
CONV_ROWS = 128

BF16 = jnp.bfloat16
F32 = jnp.float32


def _dot(a, b):
    return jnp.dot(a, b, preferred_element_type=F32)


def _dot_nt(a, b):
    return lax.dot_general(a, b, (((1,), (1,)), ((), ())), preferred_element_type=F32)


def _layer_norm(x, g, b):
    mu = jnp.mean(x, axis=-1, keepdims=True)
    xc = x - mu
    var = jnp.mean(xc * xc, axis=-1, keepdims=True)
    return xc * lax.rsqrt(var + LN_EPS) * g + b


def _rms_norm(x, g):
    return x * lax.rsqrt(jnp.mean(x * x, axis=-1, keepdims=True) + RMS_EPS) * g


def _modulate(x, m, idx):
    return x * (1.0 + m[3 * idx + 1:3 * idx + 2]) + m[3 * idx:3 * idx + 1]


def _params(n_grid):
    return pltpu.CompilerParams(dimension_semantics=("arbitrary",) * n_grid,
                                vmem_limit_bytes=V7X_VMEM_LIMIT)


def _const_spec(shape):
    return pl.BlockSpec(shape, lambda *_: (0,) * len(shape), pipeline_mode=pl.Buffered(1))


def _row_tile(length, want):
    return min(want, length)


def _adaln_kernel(c_ref, w_ref, b_ref, o_ref):
    a = jax.nn.silu(c_ref[...]).astype(BF16)
    o_ref[0] = _dot(a, w_ref[0].astype(BF16)) + b_ref[0]


def _adaln(conds, w_mod, b_mod, tn=1152):
    depth, d, n = w_mod.shape
    rows = conds.shape[0]
    return pl.pallas_call(
        _adaln_kernel,
        grid=(depth, n // tn),
        in_specs=[pl.BlockSpec((rows, d), lambda l, j: (0, 0)),
                  pl.BlockSpec((1, d, tn), lambda l, j: (l, 0, j)),
                  pl.BlockSpec((1, 1, tn), lambda l, j: (l, 0, j))],
        out_specs=pl.BlockSpec((1, rows, tn), lambda l, j: (l, 0, j)),
        out_shape=jax.ShapeDtypeStruct((depth, rows, n), F32),
        compiler_params=_params(2),
        name="adaln",
    )(conds, w_mod, b_mod.reshape(depth, 1, n))


def _ffn_kernel(x_ref, mod_ref, w13_ref, w2_ref, g_ref, b_ref, o_ref, *, idx, alpha, chunk):
    x = x_ref[0]
    m = mod_ref[0]
    xm = _modulate(x, m, idx).astype(BF16)
    acc = jnp.zeros(x.shape, F32)
    for c in range(D_FF // chunk):
        gt = _dot(xm, w13_ref[:, c * chunk:(c + 1) * chunk])
        up = _dot(xm, w13_ref[:, D_FF + c * chunk:D_FF + (c + 1) * chunk])
        h = (gt * jax.nn.sigmoid(gt) * up).astype(BF16)
        acc = acc + _dot(h, w2_ref[c * chunk:(c + 1) * chunk, :])
    r = alpha * x + (MACARON * m[3 * idx + 2:3 * idx + 3]) * acc
    o_ref[0] = _layer_norm(r, g_ref[...], b_ref[...])


def _stacked_spec(w, lead):
    shape = (None,) * len(lead) + tuple(w.shape[len(lead):])
    index = tuple(lead) + (0,) * (w.ndim - len(lead))
    return pl.BlockSpec(shape, lambda *_: index, pipeline_mode=pl.Buffered(1))


def _ffn(x, mods, mod_row, idx, w13, w2, which, g, b, alpha, tm=1024, chunk=256):
    bsz, length, d = x.shape
    tm = _row_tile(length, tm)
    kernel = functools.partial(_ffn_kernel, idx=idx, alpha=alpha, chunk=chunk)
    return pl.pallas_call(
        kernel,
        grid=(bsz, length // tm),
        in_specs=[pl.BlockSpec((1, tm, d), lambda bi, i: (bi, i, 0)),
                  pl.BlockSpec((1, N_MOD, d), lambda bi, i: (mod_row(bi), 0, 0)),
                  _stacked_spec(w13, which), _stacked_spec(w2, which),
                  _const_spec((1, d)), _const_spec((1, d))],
        out_specs=pl.BlockSpec((1, tm, d), lambda bi, i: (bi, i, 0)),
        out_shape=jax.ShapeDtypeStruct(x.shape, F32),
        compiler_params=_params(2),
        name="ffn",
    )(x, mods, w13, w2, g.reshape(1, d), b.reshape(1, d))


def _proj_kernel(x_ref, mod_ref, win_ref, qg_ref, kvg_ref, wq_ref, wk_ref, wvt_ref, c2_ref, s2_ref,
                 sg_ref, sb_ref, ws_ref, bs_ref, q_ref, k_ref, vt_ref, bg_ref):
    x = x_ref[0]
    tm = x.shape[0]
    h = _modulate(x, mod_ref[0], 1).astype(BF16)
    z = _dot(h, win_ref[...])
    o_kv = Q_RANK
    o_kr = o_kv + KV_RANK
    o_sw = o_kr + HEAD_PAD
    o_b = o_sw + HEAD_PAD
    cq, ckv = z[:, :o_kv], z[:, o_kv:o_kr]
    zkr, zsw, zb = z[:, o_kr:o_sw], z[:, o_sw:o_b], z[:, o_b:]

    c2 = c2_ref[...]
    s2 = s2_ref[...]
    width = A_HEADS * HEAD_PAD
    cqn = _rms_norm(cq, qg_ref[...]).astype(BF16)
    qab = _dot(cqn, wq_ref[...])
    q = qab[:, :width] * jnp.tile(c2, (1, A_HEADS)) + qab[:, width:] * jnp.tile(s2, (1, A_HEADS))
    q_ref[0] = (q * (ATTN_SCALE * LOG2_E)).astype(BF16)

    ckvn = _rms_norm(ckv, kvg_ref[...]).astype(BF16)
    k_rope = zkr * c2 + zsw * s2
    k = _dot(ckvn, wk_ref[...]) + jnp.tile(k_rope, (1, A_HEADS))
    k_ref[0] = k.astype(BF16)
    vt_ref[0] = _dot_nt(wvt_ref[...], ckvn).astype(BF16)

    gz = 0.5 * zb * (1.0 + lax.erf(zb * math.sqrt(0.5)))
    u, v = gz[:, :B_WIDTH], gz[:, B_WIDTH:]
    vn = _layer_norm(v, sg_ref[...], sb_ref[...]).astype(BF16)
    gch = B_WIDTH // B_GROUPS
    for n in range(tm // CHUNK):
        rows = slice(n * CHUNK, (n + 1) * CHUNK)
        for g in range(B_GROUPS):
            cols = slice(g * gch, (g + 1) * gch)
            mixed = _dot(ws_ref[g], vn[rows, cols]) + bs_ref[g]
            bg_ref[0, rows, cols] = (u[rows, cols] * mixed).astype(BF16)


def _proj(x, mods, mod_row, ew, c2, s2, tm=512):
    bsz, length, d = x.shape
    tm = _row_tile(length, tm)
    width = A_HEADS * HEAD_PAD
    row = lambda bi, i: (bi, i, 0)
    return pl.pallas_call(
        _proj_kernel,
        grid=(bsz, length // tm),
        in_specs=[pl.BlockSpec((1, tm, d), row),
                  pl.BlockSpec((1, N_MOD, d), lambda bi, i: (mod_row(bi), 0, 0)),
                  _const_spec(ew["w_in"].shape), _const_spec((1, Q_RANK)), _const_spec((1, KV_RANK)),
                  _const_spec(ew["wq"].shape), _const_spec(ew["wk"].shape), _const_spec(ew["wvt"].shape),
                  pl.BlockSpec((tm, HEAD_PAD), lambda bi, i: (i, 0)),
                  pl.BlockSpec((tm, HEAD_PAD), lambda bi, i: (i, 0)),
                  _const_spec((1, B_WIDTH)), _const_spec((1, B_WIDTH)),
                  _const_spec(ew["w_s"].shape), _const_spec(ew["b_s"].shape)],
        out_specs=[pl.BlockSpec((1, tm, width), row),
                   pl.BlockSpec((1, tm, width), row),
                   pl.BlockSpec((1, A_HEADS * V_DIM, tm), lambda bi, i: (bi, 0, i)),
                   pl.BlockSpec((1, tm, B_WIDTH), row)],
        out_shape=[jax.ShapeDtypeStruct((bsz, length, width), BF16),
                   jax.ShapeDtypeStruct((bsz, length, width), BF16),
                   jax.ShapeDtypeStruct((bsz, A_HEADS * V_DIM, length), BF16),
                   jax.ShapeDtypeStruct((bsz, length, B_WIDTH), BF16)],
        compiler_params=_params(2),
        name="proj_even",
    )(x, mods, ew["w_in"], ew["q_norm"], ew["kv_norm"], ew["wq"], ew["wk"], ew["wvt"], c2, s2,
      ew["sgu_g"], ew["sgu_b"], ew["w_s"], ew["b_s"])


def _attn_kernel(q_ref, *refs):
    o_ref = refs[-1]
    tq = q_ref.shape[1]
    gw = min(ATTN_QGROUP, tq)
    n_groups = tq // gw
    ck = ATTN_KEY_CHUNK
    chunks = [(k_ref, vt_ref, c * ck) for k_ref, vt_ref in zip(refs[0:-1:2], refs[1:-1:2])
              for c in range(k_ref.shape[1] // ck)]
    n_chunks = len(chunks)
    ones = jnp.ones((BF16_ROWS, ck), BF16)
    qs = [q_ref[0, g * gw:(g + 1) * gw, :] for g in range(n_groups)]

    def scores(c, g):
        k_ref, _, r0 = chunks[c]
        return _dot_nt(k_ref[0, r0:r0 + ck, :], qs[g])

    pending = [[scores(c, g) for c in range(min(ATTN_LOOKAHEAD, n_chunks))] for g in range(n_groups)]
    m = [None] * n_groups
    acc = [None] * n_groups
    for c in range(n_chunks):
        _, vt_ref, r0 = chunks[c]
        v_aug = jnp.concatenate([vt_ref[0, :, r0:r0 + ck], ones], axis=0)
        for g in range(n_groups):
            s = pending[g].pop(0)
            if c + ATTN_LOOKAHEAD < n_chunks:
                pending[g].append(scores(c + ATTN_LOOKAHEAD, g))
            m_c = jnp.max(s, axis=0, keepdims=True)
            m_new = m_c if m[g] is None else jnp.maximum(m[g], m_c)
            p = jnp.exp2(s - m_new).astype(BF16)
            pv = _dot(v_aug, p)
            acc[g] = pv if m[g] is None else acc[g] * jnp.exp2(m[g] - m_new) + pv
            m[g] = m_new
    for g in range(n_groups):
        o_ref[0, :, g * gw:(g + 1) * gw] = acc[g][:V_DIM] / acc[g][V_DIM:V_DIM + 1]


def _attention(q, kv_streams, tq=1024):
    bsz, lq, _ = q.shape
    tq = _row_tile(lq, tq)
    kv_specs, kv_args = [], []
    for k, vt in kv_streams:
        lk = k.shape[1]
        assert lk % ATTN_KEY_CHUNK == 0
        kv_specs += [pl.BlockSpec((1, lk, HEAD_PAD), lambda bi, h, i: (bi, 0, h)),
                     pl.BlockSpec((1, V_DIM, lk), lambda bi, h, i: (bi, h, 0))]
        kv_args += [k, vt]
    return pl.pallas_call(
        _attn_kernel,
        grid=(bsz, A_HEADS, lq // tq),
        in_specs=[pl.BlockSpec((1, tq, HEAD_PAD), lambda bi, h, i: (bi, i, h))] + kv_specs,
        out_specs=pl.BlockSpec((1, V_DIM, tq), lambda bi, h, i: (bi, h, i)),
        out_shape=jax.ShapeDtypeStruct((bsz, A_HEADS * V_DIM, lq), F32),
        compiler_params=_params(3),
        name="attention",
    )(q, *kv_args)


def _outres_kernel(x_ref, mod_ref, at_ref, bg_ref, woa_ref, wob_ref, g_ref, b_ref, o_ref, *, alpha):
    a = at_ref[0].T.astype(BF16)
    y = _dot(a, woa_ref[...]) + _dot(bg_ref[0], wob_ref[...])
    r = alpha * x_ref[0] + mod_ref[0][5:6] * y
    o_ref[0] = _layer_norm(r, g_ref[...], b_ref[...])


def _outres(x, mods, mod_row, at, bg, woa, wob, g, b, alpha, tm=512):
    bsz, length, d = x.shape
    tm = _row_tile(length, tm)
    row = lambda bi, i: (bi, i, 0)
    return pl.pallas_call(
        functools.partial(_outres_kernel, alpha=alpha),
        grid=(bsz, length // tm),
        in_specs=[pl.BlockSpec((1, tm, d), row),
                  pl.BlockSpec((1, N_MOD, d), lambda bi, i: (mod_row(bi), 0, 0)),
                  pl.BlockSpec((1, A_HEADS * V_DIM, tm), lambda bi, i: (bi, 0, i)),
                  pl.BlockSpec((1, tm, B_WIDTH), row),
                  _const_spec(woa.shape), _const_spec(wob.shape),
                  _const_spec((1, d)), _const_spec((1, d))],
        out_specs=pl.BlockSpec((1, tm, d), row),
        out_shape=jax.ShapeDtypeStruct(x.shape, F32),
        compiler_params=_params(2),
        name="outres_even",
    )(x, mods, at, bg, woa, wob, g.reshape(1, d), b.reshape(1, d))


def _conv_kernel(x_ref, xp_ref, xn_ref, mod_ref, wpw_ref, bpw_ref, wdw_ref, bdw_ref, cg_ref, cb_ref,
                 wo_ref, bo_ref, g_ref, b_ref, o_ref, y_ref, c_ref, *, alpha):
    i = pl.program_id(1)
    last = pl.num_programs(1) - 1
    x = x_ref[0]
    tm, d = x.shape
    m = mod_ref[0]
    xcat = jnp.concatenate([xp_ref[0], x, xn_ref[0]], axis=0)
    h = _modulate(xcat, m, 1).astype(BF16)
    z = _dot(h, wpw_ref[...]) + bpw_ref[...]
    y = z[:, :d] * jax.nn.sigmoid(z[:, d:])
    r = lax.broadcasted_iota(jnp.int32, (tm + 2 * CONV_HALO, 1), 0)
    pad = ((r < CONV_HALO) & (i == 0)) | ((r >= tm + CONV_HALO) & (i == last))
    y_ref[...] = jnp.where(pad, 0.0, y)

    off = CONV_HALO - CONV_W // 2
    for rb in range(tm // CONV_ROWS):
        r0 = rb * CONV_ROWS
        for lb in range(d // LANES):
            lanes = slice(lb * LANES, (lb + 1) * LANES)
            acc = jnp.zeros((CONV_ROWS, LANES), F32) + bdw_ref[:, lanes]
            for res in range(SUBLANES):
                part = None
                for a in range((CONV_W + off) // SUBLANES + 1):
                    t = a * SUBLANES + res - off
                    if 0 <= t < CONV_W:
                        rows = slice(r0 + a * SUBLANES, r0 + a * SUBLANES + CONV_ROWS + SUBLANES)
                        term = wdw_ref[t:t + 1, lanes] * y_ref[rows, lanes]
                        part = term if part is None else part + term
                acc = acc + part[res:res + CONV_ROWS]
            c_ref[r0:r0 + CONV_ROWS, lanes] = acc

    t2 = jax.nn.silu(_layer_norm(c_ref[...], cg_ref[...], cb_ref[...])).astype(BF16)
    y2 = _dot(t2, wo_ref[...]) + bo_ref[...]
    res = alpha * x + m[5:6] * y2
    o_ref[0] = _layer_norm(res, g_ref[...], b_ref[...])


def _convmix(x, mods, mod_row, ow, g, b, alpha, tm=1024):
    bsz, length, d = x.shape
    tm = _row_tile(length, tm)
    per = tm // CONV_HALO
    n_halo = length // CONV_HALO
    row = lambda bi, i: (bi, i, 0)
    return pl.pallas_call(
        functools.partial(_conv_kernel, alpha=alpha),
        grid=(bsz, length // tm),
        in_specs=[pl.BlockSpec((1, tm, d), row),
                  pl.BlockSpec((1, CONV_HALO, d), lambda bi, i: (bi, jnp.maximum(i * per - 1, 0), 0)),
                  pl.BlockSpec((1, CONV_HALO, d),
                               lambda bi, i: (bi, jnp.minimum((i + 1) * per, n_halo - 1), 0)),
                  pl.BlockSpec((1, N_MOD, d), lambda bi, i: (mod_row(bi), 0, 0)),
                  _const_spec(ow["w_pw1"].shape), _const_spec((1, 2 * d)),
                  _const_spec((CONV_W, d)), _const_spec((1, d)), _const_spec((1, d)), _const_spec((1, d)),
                  _const_spec(ow["w_out"].shape), _const_spec((1, d)),
                  _const_spec((1, d)), _const_spec((1, d))],
        out_specs=pl.BlockSpec((1, tm, d), row),
        out_shape=jax.ShapeDtypeStruct(x.shape, F32),
        scratch_shapes=[pltpu.VMEM((tm + 2 * CONV_HALO, d), F32), pltpu.VMEM((tm, d), F32)],
        compiler_params=_params(2),
        name="conv_odd",
    )(x, x, x, mods, ow["w_pw1"], ow["b_pw1"], ow["w_dw"], ow["b_dw"], ow["ln_g"], ow["ln_b"],
      ow["w_out"], ow["b_out"], g.reshape(1, d), b.reshape(1, d))


def _pad_cols(w, left, total):
    return jnp.pad(w, ((0, 0), (left, total - left - w.shape[1])))


def _even_weights(w_in, q_norm, kv_norm, w_q_up, w_kv_up, sgu_g, sgu_b, w_s, b_s, w_out):
    half = QK_ROPE // 2
    perm = jnp.concatenate([jnp.arange(0, QK_ROPE, 2), jnp.arange(1, QK_ROPE, 2)])
    swap = jnp.concatenate([perm[half:], perm[:half]])
    o_kv = Q_RANK
    o_kr = o_kv + KV_RANK
    o_b = o_kr + QK_ROPE
    kr = w_in[:, o_kr:o_b]
    w_in_ext = jnp.concatenate([w_in[:, :o_kr],
                                _pad_cols(kr[:, perm], QK_NOPE, HEAD_PAD),
                                _pad_cols(kr[:, swap], QK_NOPE, HEAD_PAD),
                                w_in[:, o_b:]], axis=1)
    wq = w_q_up.reshape(Q_RANK, A_HEADS, QK_NOPE + QK_ROPE)
    nope, rope = wq[:, :, :QK_NOPE], wq[:, :, QK_NOPE:]
    zeros_tail = jnp.zeros((Q_RANK, A_HEADS, HEAD_PAD - QK_NOPE - QK_ROPE), w_q_up.dtype)
    wq_a = jnp.concatenate([nope, rope[:, :, perm], zeros_tail], axis=2)
    wq_b = jnp.concatenate([jnp.zeros_like(nope), rope[:, :, swap], zeros_tail], axis=2)
    wq_ab = jnp.concatenate([wq_a.reshape(Q_RANK, -1), wq_b.reshape(Q_RANK, -1)], axis=1)
    wkv = w_kv_up.reshape(KV_RANK, A_HEADS, QK_NOPE + V_DIM)
    wk = jnp.concatenate([wkv[:, :, :QK_NOPE],
                          jnp.zeros((KV_RANK, A_HEADS, HEAD_PAD - QK_NOPE), w_kv_up.dtype)], axis=2)
    wvt = wkv[:, :, QK_NOPE:].reshape(KV_RANK, A_HEADS * V_DIM).T
    a_width = A_HEADS * V_DIM
    return {
        "w_in": w_in_ext.astype(BF16),
        "q_norm": q_norm.reshape(1, Q_RANK), "kv_norm": kv_norm.reshape(1, KV_RANK),
        "wq": wq_ab.astype(BF16), "wk": wk.reshape(KV_RANK, -1).astype(BF16), "wvt": wvt.astype(BF16),
        "sgu_g": sgu_g.reshape(1, B_WIDTH), "sgu_b": sgu_b.reshape(1, B_WIDTH),
        "w_s": w_s.astype(BF16),
        "b_s": jnp.broadcast_to(b_s[:, :, None], (B_GROUPS, CHUNK, B_WIDTH // B_GROUPS)),
        "wo_a": w_out[:a_width].astype(BF16), "wo_b": w_out[a_width:].astype(BF16),
    }


def _rope_tables(length):
    half = QK_ROPE // 2
    rows = jnp.repeat(jnp.arange(length // GRID_W, dtype=F32), GRID_W)
    cols = jnp.tile(jnp.arange(GRID_W, dtype=F32), length // GRID_W)
    inv = 1.0 / (ROPE_BASE ** (jnp.arange(0, half, 2, dtype=F32) / half))
    ang = jnp.concatenate([rows[:, None] * inv, cols[:, None] * inv], axis=-1)
    cos, sin = jnp.cos(ang), jnp.sin(ang)
    tail = jnp.zeros((length, HEAD_PAD - QK_NOPE - QK_ROPE), F32)
    c2 = jnp.concatenate([jnp.ones((length, QK_NOPE), F32), cos, cos, tail], axis=1)
    s2 = jnp.concatenate([jnp.zeros((length, QK_NOPE), F32), -sin, sin, tail], axis=1)
    return c2, s2


def _identity_tables(length):
    c2 = jnp.concatenate([jnp.ones((length, QK_NOPE + QK_ROPE), F32),
                          jnp.zeros((length, HEAD_PAD - QK_NOPE - QK_ROPE), F32)], axis=1)
    return c2, jnp.zeros((length, HEAD_PAD), F32)


def _ctx_needed(l, depth):
    return any(j % 2 == 0 for j in range(l, depth))


def kernel(x, c, ctx, c_ctx, w_mod, b_mod, ln_g, ln_b, ffn_w13, ffn_w2, e_w_in, e_q_norm, e_kv_norm, e_w_q_up, e_w_kv_up, e_sgu_g, e_sgu_b, e_w_s, e_b_s, e_w_out, o_w_pw1, o_b_pw1, o_w_dw, o_b_dw, o_ln_g, o_ln_b, o_w_out, o_b_out):
    bsz, length, d = x.shape
    ctx_len = ctx.shape[1]
    depth = w_mod.shape[0]
    alpha = (2 * depth) ** 0.25

    n_rows = -(-(bsz + 1) // SUBLANES) * SUBLANES
    conds = jnp.zeros((n_rows, d), F32).at[:bsz].set(c).at[bsz].set(c_ctx)
    mods = _adaln(conds, w_mod, b_mod).reshape(depth, n_rows, N_MOD, d)
    lat_row = lambda bi: bi
    ctx_row = lambda bi: bsz

    w13 = ffn_w13.astype(BF16)
    w2 = ffn_w2.astype(BF16)
    c2_lat, s2_lat = _rope_tables(length)
    c2_ctx, s2_ctx = _identity_tables(ctx_len)

    lat, cx = x, ctx
    for l in range(depth):
        ctx_in = _ctx_needed(l, depth)
        ctx_out = _ctx_needed(l + 1, depth)
        ml = mods[l]
        ffn_a = (w13, w2, (l, 0), ln_g[l, 0], ln_b[l, 0], alpha)
        ffn_b = (w13, w2, (l, 1), ln_g[l, 2], ln_b[l, 2], alpha)
        lat = _ffn(lat, ml, lat_row, 0, *ffn_a)
        if ctx_in:
            cx = _ffn(cx, ml, ctx_row, 0, *ffn_a)
        if l % 2 == 0:
            e = l // 2
            ew = _even_weights(e_w_in[e], e_q_norm[e], e_kv_norm[e], e_w_q_up[e], e_w_kv_up[e],
                               e_sgu_g[e], e_sgu_b[e], e_w_s[e], e_b_s[e], e_w_out[e])
            q_l, k_l, vt_l, bg_l = _proj(lat, ml, lat_row, ew, c2_lat, s2_lat)
            q_c, k_c, vt_c, bg_c = _proj(cx, ml, ctx_row, ew, c2_ctx, s2_ctx)
            at_l = _attention(q_l, [(k_c, vt_c), (k_l, vt_l)])
            res = (ew["wo_a"], ew["wo_b"], ln_g[l, 1], ln_b[l, 1], alpha)
            lat = _outres(lat, ml, lat_row, at_l, bg_l, *res)
            if ctx_out:
                at_c = _attention(q_c, [(k_c, vt_c)])
                cx = _outres(cx, ml, ctx_row, at_c, bg_c, *res)
        else:
            o = l // 2
            ow = {"w_pw1": o_w_pw1[o].astype(BF16), "b_pw1": o_b_pw1[o].reshape(1, -1),
                  "w_dw": o_w_dw[o], "b_dw": o_b_dw[o].reshape(1, d),
                  "ln_g": o_ln_g[o].reshape(1, d), "ln_b": o_ln_b[o].reshape(1, d),
                  "w_out": o_w_out[o].astype(BF16), "b_out": o_b_out[o].reshape(1, d)}
            lat = _convmix(lat, ml, lat_row, ow, ln_g[l, 1], ln_b[l, 1], alpha)
            if ctx_out:
                cx = _convmix(cx, ml, ctx_row, ow, ln_g[l, 1], ln_b[l, 1], alpha)
        lat = _ffn(lat, ml, lat_row, 2, *ffn_b)
        if ctx_out:
            cx = _ffn(cx, ml, ctx_row, 2, *ffn_b)
    return lat
```
